```python
import jax, jax.numpy as jnp
from jax import lax
import numpy as np

D_MODEL = 2048
BATCH = 1
SEQ = 16384
DEPTH = 2
DEC_BATCH = 8
DEC_SEQ = 32
PAST_LEN = 1024

CHUNK = 64
N_MIXERS = 2
N_CONV_LAYERS = (DEPTH + 1) // 2
N_POOL_LAYERS = DEPTH // 2
CONV_WIDTH = 3
CONV_HIST = CONV_WIDTH - 1
POOL_WINDOWS = (2, 4, 8, 16)
POOL_GROUPS = len(POOL_WINDOWS)
POOL_GC = D_MODEL // POOL_GROUPS
POOL_HIST = max(POOL_WINDOWS) - 1
D_FF = 4 * D_MODEL
EPS = 1e-6

kernel_name = "hybrid_shortconv_pool_stream_step"


def rmsnorm(x, g):
    xf = x.astype(jnp.float32)
    r = lax.rsqrt(jnp.mean(xf * xf, axis=-1, keepdims=True) + EPS)
    return (xf * r).astype(x.dtype) * g


def short_conv_mixer(z, hist, w_in, conv_w, w_out):
    L = z.shape[1]
    bch = jnp.einsum('bld,de->ble', z, w_in)
    b_gate, c_gate, h = jnp.split(bch, 3, axis=-1)
    u = c_gate * h
    u_ext = jnp.concatenate([hist.astype(u.dtype), u], axis=1)
    conv = u_ext[:, 0:L] * conv_w[0]
    for k in range(1, CONV_WIDTH):
        conv = conv + u_ext[:, k:k + L] * conv_w[k]
    y = jnp.einsum('bld,de->ble', b_gate * conv, w_out)
    return y, u_ext[:, -CONV_HIST:]


def pool_mixer(z, hist, pos0, pool_w, pool_scale):
    B, L, D = z.shape
    P = POOL_HIST
    z_ext = jnp.concatenate([hist.astype(z.dtype), z], axis=1)
    c = jnp.cumsum(z_ext.astype(jnp.float32), axis=1)
    c0 = jnp.concatenate([jnp.zeros((B, 1, D), jnp.float32), c], axis=1)
    pos = (pos0 + jnp.arange(L)).astype(jnp.float32)
    groups = []
    for g, w in enumerate(POOL_WINDOWS):
        lo, hi = g * POOL_GC, (g + 1) * POOL_GC
        s = c0[:, P + 1:P + 1 + L, lo:hi] - c0[:, P + 1 - w:P + 1 - w + L, lo:hi]
        cnt = jnp.minimum(pos + 1.0, float(w))[None, :, None]
        groups.append(s / cnt)
    pooled = jnp.concatenate(groups, axis=-1).astype(z.dtype) - z
    y = jnp.einsum('blgc,gce->blge', pooled.reshape(B, L, POOL_GROUPS, POOL_GC), pool_w)
    y = y.reshape(B, L, D) * pool_scale
    return y, z_ext[:, -P:]


def sqrelu_mlp(z, w1, w2):
    h = jax.nn.relu(jnp.einsum('bld,df->blf', z, w1))
    return jnp.einsum('blf,fd->bld', h * h, w2)


def trunk(x, conv_hist, pool_hist, pos0, mix_norm, ffn_norm, conv_w_in, conv_w, conv_w_out,
          pool_w, pool_scale, ffn_w1, ffn_w2, final_norm):
    new_conv, new_pool = [], []
    for i in range(DEPTH):
        z = rmsnorm(x, mix_norm[i])
        if i % N_MIXERS == 0:
            a = i // N_MIXERS
            y, st = short_conv_mixer(z, conv_hist[a], conv_w_in[a], conv_w[a], conv_w_out[a])
            new_conv.append(st)
        else:
            b = i // N_MIXERS
            y, st = pool_mixer(z, pool_hist[b], pos0, pool_w[b], pool_scale[b])
            new_pool.append(st)
        x = x + y
        x = x + sqrelu_mlp(rmsnorm(x, ffn_norm[i]), ffn_w1[i], ffn_w2[i])
    return rmsnorm(x, final_norm), jnp.stack(new_conv), jnp.stack(new_pool)


def setup_inputs(seed: int = 0) -> dict:
    key = jax.random.key(seed)
    ks = jax.random.split(key, 16)
    f32 = jnp.float32
    D = D_MODEL
    return {
        "x_prompt": jax.random.normal(ks[0], (BATCH, SEQ, D), f32),
        "x_sample": jax.random.normal(ks[1], (DEC_BATCH, DEC_SEQ, D), f32),
        "cache_conv": jax.random.normal(ks[2], (N_CONV_LAYERS, DEC_BATCH, CONV_HIST, D), f32),
        "cache_pool": jax.random.normal(ks[3], (N_POOL_LAYERS, DEC_BATCH, POOL_HIST, D), f32),
        "mix_norm": 1.0 + 0.02 * jax.random.normal(ks[4], (DEPTH, D), f32),
        "ffn_norm": 1.0 + 0.02 * jax.random.normal(ks[5], (DEPTH, D), f32),
        "conv_w_in": jax.random.normal(ks[6], (N_CONV_LAYERS, D, 3 * D), f32) * D ** -0.5,
        "conv_w": jax.random.normal(ks[7], (N_CONV_LAYERS, CONV_WIDTH, D), f32) * CONV_WIDTH ** -0.5,
        "conv_w_out": jax.random.normal(ks[8], (N_CONV_LAYERS, D, D), f32) * D ** -0.5,
        "pool_w": jax.random.normal(ks[9], (N_POOL_LAYERS, POOL_GROUPS, POOL_GC, POOL_GC), f32) * POOL_GC ** -0.5,
        "pool_scale": 1.0 + 0.02 * jax.random.normal(ks[10], (N_POOL_LAYERS, D), f32),
        "ffn_w1": jax.random.normal(ks[11], (DEPTH, D, D_FF), f32) * D ** -0.5,
        "ffn_w2": jax.random.normal(ks[12], (DEPTH, D_FF, D), f32) * D_FF ** -0.5,
        "final_norm": 1.0 + 0.02 * jax.random.normal(ks[13], (D,), f32),
    }


def reference(x_prompt, x_sample, cache_conv, cache_pool, mix_norm, ffn_norm, conv_w_in, conv_w,
              conv_w_out, pool_w, pool_scale, ffn_w1, ffn_w2, final_norm):
    zc = jnp.zeros((N_CONV_LAYERS, x_prompt.shape[0], CONV_HIST, D_MODEL), x_prompt.dtype)
    zp = jnp.zeros((N_POOL_LAYERS, x_prompt.shape[0], POOL_HIST, D_MODEL), x_prompt.dtype)
    y_prompt, conv_state_prompt, pool_state_prompt = trunk(
        x_prompt, zc, zp, 0, mix_norm, ffn_norm, conv_w_in, conv_w, conv_w_out,
        pool_w, pool_scale, ffn_w1, ffn_w2, final_norm)
    y_sample, conv_state_sample, pool_state_sample = trunk(
        x_sample, cache_conv, cache_pool, PAST_LEN, mix_norm, ffn_norm, conv_w_in, conv_w, conv_w_out,
        pool_w, pool_scale, ffn_w1, ffn_w2, final_norm)
    return (y_prompt, y_sample, conv_state_prompt, pool_state_prompt, conv_state_sample, pool_state_sample)
```

```python
import functools

import jax
import jax.numpy as jnp
from jax import lax
from jax.experimental import pallas as pl
from jax.experimental.pallas import tpu as pltpu

EPS = 1e-6
CONV_WIDTH = 3
CONV_HIST = CONV_WIDTH - 1
POOL_WINDOWS = (2, 4, 8, 16)
POOL_HIST = max(POOL_WINDOWS) - 1
SUBLANES = 8
CONV_PAD = SUBLANES
POOL_PAD = 16
VMEM_LIMIT = 56 * 1024 * 1024

BF16 = jnp.bfloat16
F32 = jnp.float32


def _rmsnorm(x, g):
    r = lax.rsqrt(jnp.mean(x * x, axis=-1, keepdims=True) + EPS)
    return (x * r) * g


def _dot(a, b):
    return jnp.dot(a, b, preferred_element_type=F32)


def _conv_mixer_kernel(x_ref, hist_ref, g_ref, wb_ref, wc_ref, wh_ref, cw_ref, wo_ref,
                       o_ref, st_ref, z_ref, *, nseg, seg):
    i = pl.program_id(0)
    c = pl.program_id(1)

    @pl.when(c == 0)
    def _():
        x = x_ref[...]
        z_ref[...] = _rmsnorm(x, g_ref[...]).astype(BF16)
        o_ref[...] = x

    @pl.when(i == 0)
    def _():
        st_ref[c] = hist_ref[...]

    z = z_ref[...]
    b = _dot(z, wb_ref[...])
    u = _dot(z, wc_ref[...]) * _dot(z, wh_ref[...])
    cw = cw_ref[...]
    vs = []
    for s in range(nseg):
        us = u[s * seg:(s + 1) * seg]
        ext = jnp.concatenate([st_ref[c, s], us], axis=0)
        p1 = pltpu.roll(ext, 1, 0)[CONV_PAD:]
        p2 = pltpu.roll(ext, 2, 0)[CONV_PAD:]
        conv = p2 * cw[0:1] + p1 * cw[1:2] + us * cw[2:3]
        vs.append((b[s * seg:(s + 1) * seg] * conv).astype(BF16))
        st_ref[c, s] = us[seg - CONV_PAD:]
    v = vs[0] if nseg == 1 else jnp.concatenate(vs, axis=0)
    o_ref[...] += _dot(v, wo_ref[...])


def _conv_mixer(x, hist, g, w_in, cw, w_out, *, tm, tn, nseg):
    m, d = x.shape
    seg = tm // nseg
    n_i, n_c = m // tm, d // tn
    assert n_i * tm == m and n_c * tn == d and seg * nseg == tm and seg >= CONV_PAD
    assert nseg == 1 or n_i == 1
    kern = functools.partial(_conv_mixer_kernel, nseg=nseg, seg=seg)
    y, st = pl.pallas_call(
        kern,
        grid=(n_i, n_c),
        in_specs=[
            pl.BlockSpec((tm, d), lambda i, c: (i, 0)),
            pl.BlockSpec((nseg, CONV_PAD, tn), lambda i, c: (0, 0, c)),
            pl.BlockSpec((1, d), lambda i, c: (0, 0)),
            pl.BlockSpec((d, tn), lambda i, c: (0, c)),
            pl.BlockSpec((d, tn), lambda i, c: (0, n_c + c)),
            pl.BlockSpec((d, tn), lambda i, c: (0, 2 * n_c + c)),
            pl.BlockSpec((CONV_WIDTH, tn), lambda i, c: (0, c)),
            pl.BlockSpec((tn, d), lambda i, c: (c, 0)),
        ],
        out_specs=[
            pl.BlockSpec((tm, d), lambda i, c: (i, 0)),
            pl.BlockSpec((n_c, nseg, CONV_PAD, tn), lambda i, c: (0, 0, 0, 0)),
        ],
        out_shape=[
            jax.ShapeDtypeStruct((m, d), F32),
            jax.ShapeDtypeStruct((n_c, nseg, CONV_PAD, tn), F32),
        ],
        scratch_shapes=[pltpu.VMEM((tm, d), BF16)],
        compiler_params=pltpu.CompilerParams(
            dimension_semantics=("arbitrary", "arbitrary"),
            vmem_limit_bytes=VMEM_LIMIT),
        name="conv_mixer",
    )(x, hist, g, w_in, w_in, w_in, cw, w_out)
    return y, st.transpose(1, 2, 0, 3).reshape(nseg, CONV_PAD, d)


def _pool_mixer_tile(x, z, pw_ref, ps_ref, st_ref, o_ref, *, nseg, seg, pos0, row0):
    d = x.shape[-1]
    gc = d // len(POOL_WINDOWS)
    t = lax.broadcasted_iota(jnp.int32, (seg, 1), 0)
    pos = (t + (pos0 + 1)).astype(F32) + row0.astype(F32)
    for s in range(nseg):
        rows = slice(s * seg, (s + 1) * seg)
        for gi, w in enumerate(POOL_WINDOWS):
            cols = slice(gi * gc, (gi + 1) * gc)
            zs = z[rows, cols]
            a = jnp.concatenate([st_ref[s, :, cols], zs], axis=0)
            shift = 1
            while shift < w:
                a = a + pltpu.roll(a, shift, 0)
                shift *= 2
            cnt = jnp.minimum(pos, float(w))
            pooled = a[POOL_PAD:] / cnt - zs
            y = _dot(pooled.astype(BF16), pw_ref[gi]) * ps_ref[:, cols]
            o_ref[rows, cols] = x[rows, cols] + y
        st_ref[s] = z[s * seg + seg - POOL_PAD:(s + 1) * seg]


def _ffn_kernel(*refs, pool, final, nseg, seg, pos0, tm):
    if pool:
        (x_ref, g_ref, w1_ref, w2_ref, fin_ref, mg_ref, hist_ref, pw_ref, ps_ref,
         o_ref, st_ref, z_ref) = refs
    else:
        x_ref, g_ref, w1_ref, w2_ref, fin_ref, o_ref, z_ref = refs
    i = pl.program_id(0)
    f = pl.program_id(1)
    n_f = pl.num_programs(1)

    @pl.when(f == 0)
    def _():
        x = x_ref[...]
        if pool:
            @pl.when(i == 0)
            def _():
                st_ref[...] = hist_ref[...]
            _pool_mixer_tile(x, _rmsnorm(x, mg_ref[...]), pw_ref, ps_ref, st_ref, o_ref,
                             nseg=nseg, seg=seg, pos0=pos0, row0=i * tm)
            x = o_ref[...]
        else:
            o_ref[...] = x
        z_ref[...] = _rmsnorm(x, g_ref[...]).astype(BF16)

    h = jnp.maximum(_dot(z_ref[...], w1_ref[...]), 0.0)
    o_ref[...] += _dot((h * h).astype(BF16), w2_ref[...])

    if final:
        @pl.when(f == n_f - 1)
        def _():
            o_ref[...] = _rmsnorm(o_ref[...], fin_ref[...])


def _ffn(x, g, w1, w2, fin, *, tm, tf, final, pool_args=None, nseg=1, pos0=0):
    m, d = x.shape
    d_ff = w1.shape[1]
    n_i, n_f = m // tm, d_ff // tf
    seg = tm // nseg
    assert n_i * tm == m and n_f * tf == d_ff and seg * nseg == tm and seg >= POOL_PAD
    assert nseg == 1 or n_i == 1
    pool = pool_args is not None
    in_specs = [
        pl.BlockSpec((tm, d), lambda i, f: (i, 0)),
        pl.BlockSpec((1, d), lambda i, f: (0, 0)),
        pl.BlockSpec((d, tf), lambda i, f: (0, f)),
        pl.BlockSpec((tf, d), lambda i, f: (f, 0)),
        pl.BlockSpec((1, d), lambda i, f: (0, 0)),
    ]
    out_specs = [pl.BlockSpec((tm, d), lambda i, f: (i, 0))]
    out_shape = [jax.ShapeDtypeStruct((m, d), F32)]
    scratch = [pltpu.VMEM((tm, d), BF16)]
    args = [x, g, w1, w2, fin]
    if pool:
        ng, gc = pool_args[2].shape[0], pool_args[2].shape[1]
        in_specs += [
            pl.BlockSpec((1, d), lambda i, f: (0, 0)),
            pl.BlockSpec((nseg, POOL_PAD, d), lambda i, f: (0, 0, 0)),
            pl.BlockSpec((ng, gc, gc), lambda i, f: (0, 0, 0)),
            pl.BlockSpec((1, d), lambda i, f: (0, 0)),
        ]
        out_specs.append(pl.BlockSpec((nseg, POOL_PAD, d), lambda i, f: (0, 0, 0)))
        out_shape.append(jax.ShapeDtypeStruct((nseg, POOL_PAD, d), F32))
        args += list(pool_args)
    kern = functools.partial(_ffn_kernel, pool=pool, final=final, nseg=nseg, seg=seg,
                             pos0=pos0, tm=tm)
    return pl.pallas_call(
        kern,
        grid=(n_i, n_f),
        in_specs=in_specs,
        out_specs=out_specs,
        out_shape=out_shape,
        scratch_shapes=scratch,
        compiler_params=pltpu.CompilerParams(
            dimension_semantics=("arbitrary", "arbitrary"),
            vmem_limit_bytes=VMEM_LIMIT),
        name="pool_ffn" if pool else "ffn",
    )(*args)


def _pad_hist(hist, pad):
    b, h, d = hist.shape
    return jnp.concatenate([jnp.zeros((b, pad - h, d), hist.dtype), hist], axis=1)


def _trunk(x, conv_hist, pool_hist, pos0, w, *, tm, tn, tf):
    bsz, length, d = x.shape
    rows = x.reshape(bsz * length, d)
    if bsz == 1:
        nseg = 1
    else:
        nseg, tm = bsz, bsz * length
    x1, conv_st = _conv_mixer(rows, _pad_hist(conv_hist, CONV_PAD), w["mix_g0"], w["w_in"],
                              w["conv_w"], w["w_out"], tm=tm, tn=tn, nseg=nseg)
    x2, = _ffn(x1, w["ffn_g0"], w["w1_0"], w["w2_0"], w["fin"], tm=tm, tf=tf, final=False)
    y, pool_st = _ffn(x2, w["ffn_g1"], w["w1_1"], w["w2_1"], w["fin"], tm=tm, tf=tf, final=True,
                      pool_args=(w["mix_g1"], _pad_hist(pool_hist, POOL_PAD), w["pool_w"],
                                 w["pool_scale"]),
                      nseg=nseg, pos0=pos0)
    return (y.reshape(bsz, length, d),
            conv_st[None, :, CONV_PAD - CONV_HIST:, :],
            pool_st[None, :, POOL_PAD - POOL_HIST:, :])


def kernel(x_prompt, x_sample, cache_conv, cache_pool, mix_norm, ffn_norm, conv_w_in, conv_w,
           conv_w_out, pool_w, pool_scale, ffn_w1, ffn_w2, final_norm):
    depth, d = mix_norm.shape
    assert depth == 2 and conv_w_in.shape[0] == 1 and pool_w.shape[0] == 1
    past_len = 1024
    w = {
        "mix_g0": mix_norm[0:1], "mix_g1": mix_norm[1:2],
        "ffn_g0": ffn_norm[0:1], "ffn_g1": ffn_norm[1:2],
        "fin": final_norm.reshape(1, d),
        "w_in": conv_w_in[0].astype(BF16), "conv_w": conv_w[0],
        "w_out": conv_w_out[0].astype(BF16),
        "pool_w": pool_w[0].astype(BF16), "pool_scale": pool_scale[0:1],
        "w1_0": ffn_w1[0].astype(BF16), "w2_0": ffn_w2[0].astype(BF16),
        "w1_1": ffn_w1[1].astype(BF16), "w2_1": ffn_w2[1].astype(BF16),
    }
    bp = x_prompt.shape[0]
    zc = jnp.zeros((bp, CONV_HIST, d), x_prompt.dtype)
    zp = jnp.zeros((bp, POOL_HIST, d), x_prompt.dtype)
    tiles = dict(tm=512, tn=512, tf=1024)
    y_p, conv_p, pool_p = _trunk(x_prompt, zc, zp, 0, w, **tiles)
    y_s, conv_s, pool_s = _trunk(x_sample, cache_conv[0], cache_pool[0], past_len, w, **tiles)
    return (y_p, y_s, conv_p, pool_p, conv_s, pool_s)
```

```python
import functools

import jax
import jax.numpy as jnp
from jax import lax
from jax.experimental import pallas as pl
from jax.experimental.pallas import tpu as pltpu

EPS = 1e-6
CONV_WIDTH = 3
CONV_HIST = CONV_WIDTH - 1
POOL_WINDOWS = (2, 4, 8, 16)
POOL_HIST = max(POOL_WINDOWS) - 1
SUBLANES = 8
CONV_PAD = SUBLANES
POOL_PAD = 2 * SUBLANES
VMEM_LIMIT = 56 * 1024 * 1024

BF16 = jnp.bfloat16
F32 = jnp.float32


def _rmsnorm(x, g):
    r = lax.rsqrt(jnp.mean(x * x, axis=-1, keepdims=True) + EPS)
    return (x * r) * g


def _dot(a, b):
    return jnp.dot(a, b, preferred_element_type=F32)


def _weight(w_ref, w_bf_ref):
    if w_bf_ref is None:
        return w_ref[...]
    w_bf_ref[...] = w_ref[...].astype(BF16)
    return w_bf_ref[...]


def _conv_mixer_kernel(*refs, nseg, seg, cast):
    x_ref, hist_ref, g_ref, wb_ref, wc_ref, wh_ref, cw_ref, wo_ref, o_ref, st_ref = refs[:10]
    wb_bf, wc_bf, wh_bf, wo_bf = refs[10:14] if cast else (None,) * 4
    z_ref = refs[-1]
    i = pl.program_id(0)
    c = pl.program_id(1)

    @pl.when(c == 0)
    def _():
        x = x_ref[...]
        z_ref[...] = _rmsnorm(x, g_ref[...]).astype(BF16)
        o_ref[...] = x

    @pl.when(i == 0)
    def _():
        st_ref[c] = hist_ref[...]

    wb, wc, wh, wo = (_weight(wb_ref, wb_bf), _weight(wc_ref, wc_bf), _weight(wh_ref, wh_bf),
                      _weight(wo_ref, wo_bf))
    cw = cw_ref[...]
    z = z_ref[...]
    b = _dot(z, wb)
    u = _dot(z, wc) * _dot(z, wh)
    vs = []
    for s in range(nseg):
        us = u[s * seg:(s + 1) * seg]
        ext = jnp.concatenate([st_ref[c, s], us], axis=0)
        p1 = pltpu.roll(ext, 1, 0)[CONV_PAD:]
        p2 = pltpu.roll(ext, 2, 0)[CONV_PAD:]
        conv = p2 * cw[0:1] + p1 * cw[1:2] + us * cw[2:3]
        vs.append((b[s * seg:(s + 1) * seg] * conv).astype(BF16))
        st_ref[c, s] = us[seg - CONV_PAD:]
    v = vs[0] if nseg == 1 else jnp.concatenate(vs, axis=0)
    o_ref[...] += _dot(v, wo)


def _conv_mixer(x, hist, g, w_bch, w_off, cw, w_out, *, tm, tn, nseg, cast):
    m, d = x.shape
    n_i, n_c = m // tm, d // tn
    seg = tm // nseg
    assert n_i * tm == m and n_c * tn == d and seg * nseg == tm and seg >= CONV_PAD
    assert nseg == 1 or n_i == 1
    kern = functools.partial(_conv_mixer_kernel, nseg=nseg, seg=seg, cast=cast)
    w_in_specs = [pl.BlockSpec((d, tn), lambda i, c, o=o: (0, o * n_c + c)) for o in w_off]
    out_specs = [
        pl.BlockSpec((tm, d), lambda i, c: (i, 0)),
        pl.BlockSpec((n_c, nseg, CONV_PAD, tn), lambda i, c: (0, 0, 0, 0)),
    ]
    out_shape = [
        jax.ShapeDtypeStruct((m, d), F32),
        jax.ShapeDtypeStruct((n_c, nseg, CONV_PAD, tn), F32),
    ]
    if cast:
        out_specs += [pl.BlockSpec((d, tn), lambda i, c: (0, c))] * 3
        out_specs += [pl.BlockSpec((tn, d), lambda i, c: (c, 0))]
        out_shape += [jax.ShapeDtypeStruct((d, d), BF16)] * 4
    outs = pl.pallas_call(
        kern,
        grid=(n_i, n_c),
        in_specs=[
            pl.BlockSpec((tm, d), lambda i, c: (i, 0)),
            pl.BlockSpec((nseg, CONV_PAD, tn), lambda i, c: (0, 0, c)),
            pl.BlockSpec((1, d), lambda i, c: (0, 0)),
            *w_in_specs,
            pl.BlockSpec((CONV_WIDTH, tn), lambda i, c: (0, c)),
            pl.BlockSpec((tn, d), lambda i, c: (c, 0)),
        ],
        out_specs=out_specs,
        out_shape=out_shape,
        scratch_shapes=[pltpu.VMEM((tm, d), BF16)],
        compiler_params=pltpu.CompilerParams(
            dimension_semantics=("arbitrary", "arbitrary"),
            vmem_limit_bytes=VMEM_LIMIT),
        name="conv_mixer_cast" if cast else "conv_mixer",
    )(x, hist, g, *w_bch, cw, w_out)
    st = outs[1].transpose(1, 2, 0, 3).reshape(nseg, CONV_PAD, d)
    return outs[0], st, tuple(outs[2:])


def _pool_mixer_tile(x, z, pw_ref, ps_ref, st_ref, o_ref, *, nseg, seg, pos0, row0):
    d = x.shape[-1]
    gc = d // len(POOL_WINDOWS)
    t = lax.broadcasted_iota(jnp.int32, (seg, 1), 0)
    pos = (t + (pos0 + 1)).astype(F32) + row0.astype(F32)
    for s in range(nseg):
        rows = slice(s * seg, (s + 1) * seg)
        for gi, w in enumerate(POOL_WINDOWS):
            cols = slice(gi * gc, (gi + 1) * gc)
            zs = z[rows, cols]
            a = jnp.concatenate([st_ref[s, :, cols], zs], axis=0)
            shift = 1
            while shift < min(w, SUBLANES):
                a = a + pltpu.roll(a, shift, 0)
                shift *= 2
            a = a[SUBLANES:]
            a = a[SUBLANES:] + a[:seg] if w > SUBLANES else a[SUBLANES:]
            inv_cnt = 1.0 / jnp.minimum(pos, float(w))
            pooled = a * inv_cnt - zs
            y = _dot(pooled.astype(BF16), pw_ref[gi]) * ps_ref[:, cols]
            o_ref[rows, cols] = x[rows, cols] + y
        st_ref[s] = z[s * seg + seg - POOL_PAD:(s + 1) * seg]


def _ffn_kernel(*refs, pool, final, nseg, seg, pos0, tm, cast):
    x_ref, g_ref, w1_ref, w2_ref, fin_ref = refs[:5]
    n_in = 5
    if pool:
        mg_ref, hist_ref, pw_ref, ps_ref = refs[5:9]
        n_in = 9
    o_ref = refs[n_in]
    n_out = 1
    if pool:
        st_ref = refs[n_in + 1]
        n_out = 2
    w1_bf = w2_bf = pw_bf = None
    if cast:
        w1_bf, w2_bf = refs[n_in + n_out:n_in + n_out + 2]
        if pool:
            pw_bf = refs[n_in + n_out + 2]
    z_ref = refs[-1]
    i = pl.program_id(0)
    f = pl.program_id(1)
    n_f = pl.num_programs(1)

    @pl.when(f == 0)
    def _():
        x = x_ref[...]
        if pool:
            @pl.when(i == 0)
            def _():
                st_ref[...] = hist_ref[...]
                if cast:
                    pw_bf[...] = pw_ref[...].astype(BF16)
            _pool_mixer_tile(x, _rmsnorm(x, mg_ref[...]), pw_bf if cast else pw_ref, ps_ref,
                             st_ref, o_ref, nseg=nseg, seg=seg, pos0=pos0, row0=i * tm)
            x = o_ref[...]
        else:
            o_ref[...] = x
        z_ref[...] = _rmsnorm(x, g_ref[...]).astype(BF16)

    w1, w2 = _weight(w1_ref, w1_bf), _weight(w2_ref, w2_bf)
    h = jnp.maximum(_dot(z_ref[...], w1), 0.0)
    o_ref[...] += _dot((h * h).astype(BF16), w2)

    if final:
        @pl.when(f == n_f - 1)
        def _():
            o_ref[...] = _rmsnorm(o_ref[...], fin_ref[...])


def _ffn(x, g, w1, w2, layer, fin, *, tm, tf, final, cast, pool_args=None, nseg=1, pos0=0):
    m, d = x.shape
    d_ff = w1.shape[2]
    n_i, n_f = m // tm, d_ff // tf
    seg = tm // nseg
    assert n_i * tm == m and n_f * tf == d_ff and seg * nseg == tm and seg >= POOL_PAD
    assert nseg == 1 or n_i == 1
    pool = pool_args is not None
    in_specs = [
        pl.BlockSpec((tm, d), lambda i, f: (i, 0)),
        pl.BlockSpec((1, d), lambda i, f: (0, 0)),
        pl.BlockSpec((None, d, tf), lambda i, f: (layer, 0, f)),
        pl.BlockSpec((None, tf, d), lambda i, f: (layer, f, 0)),
        pl.BlockSpec((1, d), lambda i, f: (0, 0)),
    ]
    out_specs = [pl.BlockSpec((tm, d), lambda i, f: (i, 0))]
    out_shape = [jax.ShapeDtypeStruct((m, d), F32)]
    args = [x, g, w1, w2, fin]
    if pool:
        pw_shape = pool_args[2].shape
        in_specs += [
            pl.BlockSpec((1, d), lambda i, f: (0, 0)),
            pl.BlockSpec((nseg, POOL_PAD, d), lambda i, f: (0, 0, 0)),
            pl.BlockSpec(pw_shape, lambda i, f: (0, 0, 0)),
            pl.BlockSpec((1, d), lambda i, f: (0, 0)),
        ]
        out_specs.append(pl.BlockSpec((nseg, POOL_PAD, d), lambda i, f: (0, 0, 0)))
        out_shape.append(jax.ShapeDtypeStruct((nseg, POOL_PAD, d), F32))
        args += list(pool_args)
    if cast:
        out_specs += [pl.BlockSpec((None, d, tf), lambda i, f: (0, 0, f)),
                      pl.BlockSpec((None, tf, d), lambda i, f: (0, f, 0))]
        out_shape += [jax.ShapeDtypeStruct((1, d, d_ff), BF16),
                      jax.ShapeDtypeStruct((1, d_ff, d), BF16)]
        if pool:
            out_specs.append(pl.BlockSpec(pw_shape, lambda i, f: (0, 0, 0)))
            out_shape.append(jax.ShapeDtypeStruct(pw_shape, BF16))
    kern = functools.partial(_ffn_kernel, pool=pool, final=final, nseg=nseg, seg=seg,
                             pos0=pos0, tm=tm, cast=cast)
    return pl.pallas_call(
        kern,
        grid=(n_i, n_f),
        in_specs=in_specs,
        out_specs=out_specs,
        out_shape=out_shape,
        scratch_shapes=[pltpu.VMEM((tm, d), BF16)],
        compiler_params=pltpu.CompilerParams(
            dimension_semantics=("arbitrary", "arbitrary"),
            vmem_limit_bytes=VMEM_LIMIT),
        name=("pool_ffn" if pool else "ffn") + ("_cast" if cast else ""),
    )(*args)


def _pad_hist(hist, pad):
    b, h, d = hist.shape
    return jnp.concatenate([jnp.zeros((b, pad - h, d), hist.dtype), hist], axis=1)


def _trunk(x, conv_hist, pool_hist, pos0, w, *, tm, tn, tf, cast):
    bsz, length, d = x.shape
    rows = x.reshape(bsz * length, d)
    nseg = bsz
    if bsz > 1:
        tm = bsz * length
    x1, conv_st, w_conv = _conv_mixer(
        rows, _pad_hist(conv_hist, CONV_PAD), w["mix_g0"], w["w_bch"], w["w_bch_off"],
        w["conv_w"], w["w_out"], tm=tm, tn=tn, nseg=nseg, cast=cast)
    x2, *w_ffn0 = _ffn(x1, w["ffn_g0"], *w["ffn0"], w["fin"], tm=tm, tf=tf,
                       final=False, cast=cast)
    y, pool_st, *w_ffn1 = _ffn(
        x2, w["ffn_g1"], *w["ffn1"], w["fin"], tm=tm, tf=tf, final=True,
        cast=cast, pool_args=(w["mix_g1"], _pad_hist(pool_hist, POOL_PAD), w["pool_w"],
                              w["pool_scale"]),
        nseg=nseg, pos0=pos0)
    outs = (y.reshape(bsz, length, d),
            conv_st[None, :, CONV_PAD - CONV_HIST:, :],
            pool_st[None, :, POOL_PAD - POOL_HIST:, :])
    if not cast:
        return outs, None
    w_bf = dict(w, w_bch=w_conv[:3], w_bch_off=(0, 0, 0), w_out=w_conv[3],
                ffn0=(w_ffn0[0], w_ffn0[1], 0), ffn1=(w_ffn1[0], w_ffn1[1], 0),
                pool_w=w_ffn1[2])
    return outs, w_bf


def kernel(x_prompt, x_sample, cache_conv, cache_pool, mix_norm, ffn_norm, conv_w_in, conv_w,
           conv_w_out, pool_w, pool_scale, ffn_w1, ffn_w2, final_norm):
    depth, d = mix_norm.shape
    assert depth == 2 and conv_w_in.shape[0] == 1 and pool_w.shape[0] == 1
    assert x_prompt.shape[0] == 1
    past_len = 1024
    w = {
        "mix_g0": mix_norm[0:1], "mix_g1": mix_norm[1:2],
        "ffn_g0": ffn_norm[0:1], "ffn_g1": ffn_norm[1:2],
        "fin": final_norm.reshape(1, d),
        "w_bch": (conv_w_in[0],) * 3, "w_bch_off": (0, 1, 2), "conv_w": conv_w[0],
        "w_out": conv_w_out[0],
        "pool_w": pool_w[0], "pool_scale": pool_scale[0:1],
        "ffn0": (ffn_w1, ffn_w2, 0), "ffn1": (ffn_w1, ffn_w2, 1),
    }
    (y_s, conv_s, pool_s), w_bf = _trunk(
        x_sample, cache_conv[0], cache_pool[0], past_len, w,
        tm=None, tn=256, tf=512, cast=True)
    bp = x_prompt.shape[0]
    zc = jnp.zeros((bp, CONV_HIST, d), x_prompt.dtype)
    zp = jnp.zeros((bp, POOL_HIST, d), x_prompt.dtype)
    (y_p, conv_p, pool_p), _ = _trunk(
        x_prompt, zc, zp, 0, w_bf, tm=512, tn=512, tf=1024, cast=False)
    return (y_p, y_s, conv_p, pool_p, conv_s, pool_s)
```

```python
import functools

import jax
import jax.numpy as jnp
from jax import lax
from jax.experimental import pallas as pl
from jax.experimental.pallas import tpu as pltpu

EPS = 1e-6
CONV_WIDTH = 3
CONV_HIST = CONV_WIDTH - 1
POOL_WINDOWS = (2, 4, 8, 16)
POOL_HIST = max(POOL_WINDOWS) - 1
PAST_LEN = 1024
SUBLANES = 8
CONV_PAD = SUBLANES
POOL_PAD = 2 * SUBLANES
VMEM_LIMIT = 56 * 1024 * 1024

BF16 = jnp.bfloat16
F32 = jnp.float32
_PARAMS = pltpu.CompilerParams(dimension_semantics=("arbitrary", "arbitrary"),
                               vmem_limit_bytes=VMEM_LIMIT)
_ONCE = pl.Buffered(1)


def _rmsnorm(x, g):
    r = lax.rsqrt(jnp.mean(x * x, axis=-1, keepdims=True) + EPS)
    return (x * r) * g


def _dot(a, b):
    return jnp.dot(a, b, preferred_element_type=F32)


def _conv_gate(b, u, hist, cw):
    ext = jnp.concatenate([hist, u], axis=0)
    p1 = pltpu.roll(ext, 1, 0)[CONV_PAD:]
    p2 = pltpu.roll(ext, 2, 0)[CONV_PAD:]
    conv = p2 * cw[0:1] + p1 * cw[1:2] + u * cw[2:3]
    return (b * conv).astype(BF16)


def _pool_mix(x, z, hist, frames, pw_ref, ps_ref):
    r, d = x.shape
    gc = d // len(POOL_WINDOWS)
    outs = []
    for gi, w in enumerate(POOL_WINDOWS):
        cols = slice(gi * gc, (gi + 1) * gc)
        zs = z[:, cols]
        a = jnp.concatenate([hist[:, cols], zs], axis=0)
        shift = 1
        while shift < min(w, SUBLANES):
            a = a + pltpu.roll(a, shift, 0)
            shift *= 2
        a = a[SUBLANES:]
        a = a[SUBLANES:] + a[:r] if w > SUBLANES else a[SUBLANES:]
        pooled = a * (1.0 / jnp.minimum(frames, float(w))) - zs
        y = _dot(pooled.astype(BF16), pw_ref[gi]) * ps_ref[:, cols]
        outs.append(x[:, cols] + y)
    return jnp.concatenate(outs, axis=1)


def _cast(w_ref, w_bf_ref):
    w_bf_ref[...] = w_ref[...].astype(BF16)
    return w_bf_ref[...]


def _conv_sample_kernel(x_ref, hist_ref, g_ref, wb_ref, wc_ref, wh_ref, cw_ref, wo_ref,
                        o_ref, st_ref, wb_bf, wc_bf, wh_bf, wo_bf, z_ref, *, nseg, seg):
    c = pl.program_id(1)

    @pl.when(c == 0)
    def _():
        x = x_ref[...]
        z_ref[...] = _rmsnorm(x, g_ref[...]).astype(BF16)
        o_ref[...] = x

    z = z_ref[...]
    b = _dot(z, _cast(wb_ref, wb_bf))
    u = _dot(z, _cast(wc_ref, wc_bf)) * _dot(z, _cast(wh_ref, wh_bf))
    cw = cw_ref[...]
    vs = []
    for s in range(nseg):
        rows = slice(s * seg, (s + 1) * seg)
        vs.append(_conv_gate(b[rows], u[rows], hist_ref[s], cw))
        st_ref[s] = u[(s + 1) * seg - CONV_PAD:(s + 1) * seg]
    o_ref[...] += _dot(jnp.concatenate(vs, axis=0), _cast(wo_ref, wo_bf))


def _conv_sample(x, hist, g, w_in, cw, w_out, *, tn, nseg):
    m, d = x.shape
    n_c, seg = d // tn, m // nseg
    assert n_c * tn == d and seg * nseg == m and seg >= CONV_PAD
    st_spec = pl.BlockSpec((nseg, CONV_PAD, tn), lambda i, c: (0, 0, c))
    return pl.pallas_call(
        functools.partial(_conv_sample_kernel, nseg=nseg, seg=seg),
        grid=(1, n_c),
        in_specs=[
            pl.BlockSpec((m, d), lambda i, c: (0, 0)),
            st_spec,
            pl.BlockSpec((1, d), lambda i, c: (0, 0)),
            pl.BlockSpec((d, tn), lambda i, c: (0, c)),
            pl.BlockSpec((d, tn), lambda i, c: (0, n_c + c)),
            pl.BlockSpec((d, tn), lambda i, c: (0, 2 * n_c + c)),
            pl.BlockSpec((CONV_WIDTH, tn), lambda i, c: (0, c)),
            pl.BlockSpec((tn, d), lambda i, c: (c, 0)),
        ],
        out_specs=[
            pl.BlockSpec((m, d), lambda i, c: (0, 0)),
            st_spec,
            *[pl.BlockSpec((d, tn), lambda i, c: (0, c))] * 3,
            pl.BlockSpec((tn, d), lambda i, c: (c, 0)),
        ],
        out_shape=[
            jax.ShapeDtypeStruct((m, d), F32),
            jax.ShapeDtypeStruct((nseg, CONV_PAD, d), F32),
            *[jax.ShapeDtypeStruct((d, d), BF16)] * 4,
        ],
        scratch_shapes=[pltpu.VMEM((m, d), BF16)],
        compiler_params=_PARAMS,
        name="conv_sample",
    )(x, hist, g, w_in, w_in, w_in, cw, w_out)


def _ffn_sample_kernel(*refs, pool, final, nseg, seg, pos0):
    x_ref, g_ref, w1_ref, w2_ref, fin_ref = refs[:5]
    if pool:
        mg_ref, hist_ref, pw_ref, ps_ref, o_ref, st_ref, w1_bf, w2_bf, pw_bf, z_ref = refs[5:]
    else:
        o_ref, w1_bf, w2_bf, z_ref = refs[5:]
    f = pl.program_id(1)

    @pl.when(f == 0)
    def _():
        x = x_ref[...]
        if pool:
            pw_bf[...] = pw_ref[...].astype(BF16)
            zm = _rmsnorm(x, mg_ref[...])
            frames = (lax.broadcasted_iota(jnp.int32, (seg, 1), 0) + (pos0 + 1)).astype(F32)
            xs = []
            for s in range(nseg):
                rows = slice(s * seg, (s + 1) * seg)
                xs.append(_pool_mix(x[rows], zm[rows], hist_ref[s], frames, pw_bf, ps_ref))
                st_ref[s] = zm[(s + 1) * seg - POOL_PAD:(s + 1) * seg]
            x = jnp.concatenate(xs, axis=0)
        o_ref[...] = x
        z_ref[...] = _rmsnorm(x, g_ref[...]).astype(BF16)

    h = jnp.maximum(_dot(z_ref[...], _cast(w1_ref, w1_bf)), 0.0)
    o_ref[...] += _dot((h * h).astype(BF16), _cast(w2_ref, w2_bf))

    if final:
        @pl.when(f == pl.num_programs(1) - 1)
        def _():
            o_ref[...] = _rmsnorm(o_ref[...], fin_ref[...])


def _ffn_sample(x, g, w1, w2, layer, fin, *, tf, final, nseg, pool_args=None, pos0=0):
    m, d = x.shape
    d_ff = w1.shape[2]
    n_f, seg = d_ff // tf, m // nseg
    assert n_f * tf == d_ff and seg * nseg == m and seg >= POOL_PAD
    pool = pool_args is not None
    row = pl.BlockSpec((1, d), lambda i, f: (0, 0))
    in_specs = [
        pl.BlockSpec((m, d), lambda i, f: (0, 0)),
        row,
        pl.BlockSpec((None, d, tf), lambda i, f: (layer, 0, f)),
        pl.BlockSpec((None, tf, d), lambda i, f: (layer, f, 0)),
        row,
    ]
    out_specs = [pl.BlockSpec((m, d), lambda i, f: (0, 0))]
    out_shape = [jax.ShapeDtypeStruct((m, d), F32)]
    args = [x, g, w1, w2, fin]
    if pool:
        pw_shape = pool_args[2].shape
        st_spec = pl.BlockSpec((nseg, POOL_PAD, d), lambda i, f: (0, 0, 0))
        in_specs += [row, st_spec, pl.BlockSpec(pw_shape, lambda i, f: (0, 0, 0)), row]
        out_specs.append(st_spec)
        out_shape.append(jax.ShapeDtypeStruct((nseg, POOL_PAD, d), F32))
        args += list(pool_args)
    out_specs += [pl.BlockSpec((None, d, tf), lambda i, f: (0, 0, f)),
                  pl.BlockSpec((None, tf, d), lambda i, f: (0, f, 0))]
    out_shape += [jax.ShapeDtypeStruct((1, d, d_ff), BF16),
                  jax.ShapeDtypeStruct((1, d_ff, d), BF16)]
    if pool:
        out_specs.append(pl.BlockSpec(pw_shape, lambda i, f: (0, 0, 0)))
        out_shape.append(jax.ShapeDtypeStruct(pw_shape, BF16))
    return pl.pallas_call(
        functools.partial(_ffn_sample_kernel, pool=pool, final=final, nseg=nseg, seg=seg,
                          pos0=pos0),
        grid=(1, n_f),
        in_specs=in_specs,
        out_specs=out_specs,
        out_shape=out_shape,
        scratch_shapes=[pltpu.VMEM((m, d), BF16)],
        compiler_params=_PARAMS,
        name="pool_ffn_sample" if pool else "ffn_sample",
    )(*args)


def _next_rows_spec(rc, d, n_i, n_k):
    last = n_i * n_k - 1
    return pl.BlockSpec((rc, d), lambda i, k: (jnp.minimum((i + 1) * n_k + k, last), 0))


def _conv_prompt_kernel(xf_ref, xn_ref, hist_ref, g_ref, wb_ref, wc_ref, wh_ref, cw_ref, wo_ref,
                        o_ref, st_ref, z0_ref, z1_ref, r0_ref, r1_ref, *, rc):
    i = pl.program_id(0)
    c = pl.program_id(1)

    @pl.when((i == 0) & (c == 0))
    def _():
        x = xf_ref[...]
        r0_ref[...] = x
        z0_ref[...] = _rmsnorm(x, g_ref[...]).astype(BF16)

    @pl.when(i == 0)
    def _():
        st_ref[c] = hist_ref[0]

    def step(z_ref, res_ref, z_next_ref, res_next_ref):
        rows = pl.ds(pl.multiple_of(c * rc, rc), rc)
        xn = xn_ref[...]
        res_next_ref[rows] = xn
        z_next_ref[rows] = _rmsnorm(xn, g_ref[...]).astype(BF16)

        z = z_ref[...]
        b = _dot(z, wb_ref[...])
        u = _dot(z, wc_ref[...]) * _dot(z, wh_ref[...])
        v = _conv_gate(b, u, st_ref[c], cw_ref[...])
        st_ref[c] = u[u.shape[0] - CONV_PAD:]
        new = res_ref[...] + _dot(v, wo_ref[...])
        res_ref[...] = new
        o_ref[...] = new

    pl.when(i % 2 == 0)(lambda: step(z0_ref, r0_ref, z1_ref, r1_ref))
    pl.when(i % 2 == 1)(lambda: step(z1_ref, r1_ref, z0_ref, r0_ref))


def _conv_prompt(x, hist, g, wb, wc, wh, cw, w_out, *, tm, tn):
    m, d = x.shape
    n_i, n_c = m // tm, d // tn
    rc = tm // n_c
    assert n_i * tm == m and n_c * tn == d and rc * n_c == tm and rc % (2 * SUBLANES) == 0
    row = pl.BlockSpec((1, d), lambda i, c: (0, 0))
    w_spec = pl.BlockSpec((d, tn), lambda i, c: (0, c))
    y, st = pl.pallas_call(
        functools.partial(_conv_prompt_kernel, rc=rc),
        grid=(n_i, n_c),
        in_specs=[
            pl.BlockSpec((tm, d), lambda i, c: (0, 0), pipeline_mode=_ONCE),
            _next_rows_spec(rc, d, n_i, n_c),
            pl.BlockSpec((1, CONV_PAD, tn), lambda i, c: (0, 0, c)),
            row, w_spec, w_spec, w_spec,
            pl.BlockSpec((CONV_WIDTH, tn), lambda i, c: (0, c)),
            pl.BlockSpec((tn, d), lambda i, c: (c, 0)),
        ],
        out_specs=[
            pl.BlockSpec((tm, d), lambda i, c: (i, 0)),
            pl.BlockSpec((n_c, CONV_PAD, tn), lambda i, c: (0, 0, 0)),
        ],
        out_shape=[
            jax.ShapeDtypeStruct((m, d), F32),
            jax.ShapeDtypeStruct((n_c, CONV_PAD, tn), F32),
        ],
        scratch_shapes=[pltpu.VMEM((tm, d), BF16)] * 2 + [pltpu.VMEM((tm, d), F32)] * 2,
        compiler_params=_PARAMS,
        name="conv_prompt",
    )(x, x, hist, g, wb, wc, wh, cw, w_out)
    return y, st.transpose(1, 0, 2).reshape(1, CONV_PAD, d)


def _ffn_prompt_kernel(*refs, pool, final, n_i, tm, rc):
    xf_ref, xn_ref, g_ref, w1_ref, w2_ref, fin_ref = refs[:6]
    if pool:
        mg_ref, hist_ref, pw_ref, ps_ref, o_ref, st_ref = refs[6:12]
    else:
        o_ref = refs[6]
    z0_ref, z1_ref, r0_ref, r1_ref = refs[-4:]
    i = pl.program_id(0)
    f = pl.program_id(1)
    n_f = pl.num_programs(1)
    t = lax.broadcasted_iota(jnp.int32, (rc, 1), 0)

    def prepare(x, row0, z_ref, res_ref, rows, keep_state):
        if pool:
            zm = _rmsnorm(x, mg_ref[...])
            frames = (t + (row0 + 1)).astype(F32)
            x = _pool_mix(x, zm, st_ref[0], frames, pw_ref, ps_ref)
            tail = zm[rc - POOL_PAD:]
            st_ref[0] = tail if keep_state is None else jnp.where(keep_state, st_ref[0], tail)
        res_ref[rows] = x
        z_ref[rows] = _rmsnorm(x, g_ref[...]).astype(BF16)

    @pl.when((i == 0) & (f == 0))
    def _():
        if pool:
            st_ref[...] = hist_ref[...]
        for k in range(tm // rc):
            prepare(xf_ref[k * rc:(k + 1) * rc], k * rc, z0_ref, r0_ref,
                    slice(k * rc, (k + 1) * rc), None)

    def step(z_ref, res_ref, z_next_ref, res_next_ref):
        prepare(xn_ref[...], (i + 1) * tm + f * rc, z_next_ref, res_next_ref,
                pl.ds(pl.multiple_of(f * rc, rc), rc), i == n_i - 1)

        h = jnp.maximum(_dot(z_ref[...], w1_ref[...]), 0.0)
        new = res_ref[...] + _dot((h * h).astype(BF16), w2_ref[...])
        res_ref[...] = new
        if final:
            @pl.when(f == n_f - 1)
            def _():
                o_ref[...] = _rmsnorm(res_ref[...], fin_ref[...])
        else:
            o_ref[...] = new

    pl.when(i % 2 == 0)(lambda: step(z0_ref, r0_ref, z1_ref, r1_ref))
    pl.when(i % 2 == 1)(lambda: step(z1_ref, r1_ref, z0_ref, r0_ref))


def _ffn_prompt(x, g, w1, w2, fin, *, tm, tf, final, pool_args=None):
    m, d = x.shape
    d_ff = w1.shape[2]
    n_i, n_f = m // tm, d_ff // tf
    rc = tm // n_f
    assert n_i * tm == m and n_f * tf == d_ff and rc * n_f == tm
    assert rc % (2 * SUBLANES) == 0 and rc >= POOL_PAD
    pool = pool_args is not None
    row = pl.BlockSpec((1, d), lambda i, f: (0, 0))
    in_specs = [
        pl.BlockSpec((tm, d), lambda i, f: (0, 0), pipeline_mode=_ONCE),
        _next_rows_spec(rc, d, n_i, n_f),
        row,
        pl.BlockSpec((None, d, tf), lambda i, f: (0, 0, f)),
        pl.BlockSpec((None, tf, d), lambda i, f: (0, f, 0)),
        row,
    ]
    out_specs = [pl.BlockSpec((tm, d), lambda i, f: (i, 0))]
    out_shape = [jax.ShapeDtypeStruct((m, d), F32)]
    args = [x, x, g, w1, w2, fin]
    if pool:
        st_spec = pl.BlockSpec((1, POOL_PAD, d), lambda i, f: (0, 0, 0))
        in_specs += [row, st_spec,
                     pl.BlockSpec(pool_args[2].shape, lambda i, f: (0, 0, 0), pipeline_mode=_ONCE),
                     row]
        out_specs.append(st_spec)
        out_shape.append(jax.ShapeDtypeStruct((1, POOL_PAD, d), F32))
        args += list(pool_args)
    return pl.pallas_call(
        functools.partial(_ffn_prompt_kernel, pool=pool, final=final, n_i=n_i, tm=tm, rc=rc),
        grid=(n_i, n_f),
        in_specs=in_specs,
        out_specs=out_specs,
        out_shape=out_shape,
        scratch_shapes=[pltpu.VMEM((tm, d), BF16)] * 2 + [pltpu.VMEM((tm, d), F32)] * 2,
        compiler_params=_PARAMS,
        name="pool_ffn_prompt" if pool else "ffn_prompt",
    )(*args)


def _pad_hist(hist, pad):
    b, h, d = hist.shape
    return jnp.concatenate([jnp.zeros((b, pad - h, d), hist.dtype), hist], axis=1)


def _states(conv_st, pool_st):
    return (conv_st[None, :, CONV_PAD - CONV_HIST:, :], pool_st[None, :, POOL_PAD - POOL_HIST:, :])


def kernel(x_prompt, x_sample, cache_conv, cache_pool, mix_norm, ffn_norm, conv_w_in, conv_w,
           conv_w_out, pool_w, pool_scale, ffn_w1, ffn_w2, final_norm):
    depth, d = mix_norm.shape
    assert depth == 2 and conv_w_in.shape[0] == 1 and pool_w.shape[0] == 1
    bp, lp, _ = x_prompt.shape
    bs, ls, _ = x_sample.shape
    assert bp == 1
    mix_g0, mix_g1 = mix_norm[0:1], mix_norm[1:2]
    ffn_g0, ffn_g1 = ffn_norm[0:1], ffn_norm[1:2]
    fin = final_norm.reshape(1, d)
    ps = pool_scale[0:1]

    x1, conv_s, wb, wc, wh, wo = _conv_sample(
        x_sample.reshape(bs * ls, d), _pad_hist(cache_conv[0], CONV_PAD), mix_g0, conv_w_in[0],
        conv_w[0], conv_w_out[0], tn=256, nseg=bs)
    x2, w1_0, w2_0 = _ffn_sample(x1, ffn_g0, ffn_w1, ffn_w2, 0, fin, tf=512, final=False, nseg=bs)
    y_s, pool_s, w1_1, w2_1, pw = _ffn_sample(
        x2, ffn_g1, ffn_w1, ffn_w2, 1, fin, tf=512, final=True, nseg=bs,
        pool_args=(mix_g1, _pad_hist(cache_pool[0], POOL_PAD), pool_w[0], ps), pos0=PAST_LEN)

    tm = 512
    x1, conv_p = _conv_prompt(x_prompt.reshape(lp, d), jnp.zeros((1, CONV_PAD, d), F32), mix_g0,
                              wb, wc, wh, conv_w[0], wo, tm=tm, tn=512)
    x2, = _ffn_prompt(x1, ffn_g0, w1_0, w2_0, fin, tm=tm, tf=1024, final=False)
    y_p, pool_p = _ffn_prompt(x2, ffn_g1, w1_1, w2_1, fin, tm=tm, tf=1024, final=True,
                              pool_args=(mix_g1, jnp.zeros((1, POOL_PAD, d), F32), pw, ps))

    conv_state_p, pool_state_p = _states(conv_p, pool_p)
    conv_state_s, pool_state_s = _states(conv_s, pool_s)
    return (y_p.reshape(bp, lp, d), y_s.reshape(bs, ls, d), conv_state_p, pool_state_p,
            conv_state_s, pool_state_s)
```

```python
import functools

import jax
import jax.numpy as jnp
from jax import lax
from jax.experimental import pallas as pl
from jax.experimental.pallas import tpu as pltpu

EPS = 1e-6
CONV_WIDTH = 3
CONV_HIST = CONV_WIDTH - 1
POOL_WINDOWS = (2, 4, 8, 16)
POOL_HIST = max(POOL_WINDOWS) - 1
PAST_LEN = 1024
SUBLANES = 8
CONV_PAD = SUBLANES
POOL_PAD = 2 * SUBLANES
VMEM_LIMIT = 56 * 1024 * 1024

BF16 = jnp.bfloat16
F32 = jnp.float32
_PARAMS = pltpu.CompilerParams(dimension_semantics=("arbitrary", "arbitrary"),
                               vmem_limit_bytes=VMEM_LIMIT)
_ONCE = pl.Buffered(1)


def _rmsnorm(x, g):
    r = lax.rsqrt(jnp.mean(x * x, axis=-1, keepdims=True) + EPS)
    return (x * r) * g


def _dot(a, b):
    return jnp.dot(a, b, preferred_element_type=F32)


def _conv_gate(b, u, hist, cw):
    ext = jnp.concatenate([hist, u], axis=0)
    p1 = pltpu.roll(ext, 1, 0)[CONV_PAD:]
    p2 = pltpu.roll(ext, 2, 0)[CONV_PAD:]
    conv = p2 * cw[0:1] + p1 * cw[1:2] + u * cw[2:3]
    return (b * conv).astype(BF16)


def _pool_mix(x, z, hist, frames, pw_ref, ps_ref):
    r, d = x.shape
    gc = d // len(POOL_WINDOWS)
    outs = []
    for gi, w in enumerate(POOL_WINDOWS):
        cols = slice(gi * gc, (gi + 1) * gc)
        zs = z[:, cols]
        a = jnp.concatenate([hist[:, cols], zs], axis=0)
        shift = 1
        while shift < min(w, SUBLANES):
            a = a + pltpu.roll(a, shift, 0)
            shift *= 2
        a = a[SUBLANES:]
        a = a[SUBLANES:] + a[:r] if w > SUBLANES else a[SUBLANES:]
        pooled = a * (1.0 / jnp.minimum(frames, float(w))) - zs
        y = _dot(pooled.astype(BF16), pw_ref[gi]) * ps_ref[:, cols]
        outs.append(x[:, cols] + y)
    return jnp.concatenate(outs, axis=1)


def _cast(w_ref, w_bf_ref):
    w_bf_ref[...] = w_ref[...].astype(BF16)
    return w_bf_ref[...]


def _conv_sample_kernel(x_ref, hist_ref, g_ref, wb_ref, wc_ref, wh_ref, cw_ref, wo_ref,
                        o_ref, st_ref, wb_bf, wc_bf, wh_bf, wo_bf, z_ref, *, nseg, seg):
    c = pl.program_id(1)

    @pl.when(c == 0)
    def _():
        x = x_ref[...]
        z_ref[...] = _rmsnorm(x, g_ref[...]).astype(BF16)
        o_ref[...] = x

    z = z_ref[...]
    b = _dot(z, _cast(wb_ref, wb_bf))
    u = _dot(z, _cast(wc_ref, wc_bf)) * _dot(z, _cast(wh_ref, wh_bf))
    cw = cw_ref[...]
    vs = []
    for s in range(nseg):
        rows = slice(s * seg, (s + 1) * seg)
        vs.append(_conv_gate(b[rows], u[rows], hist_ref[s], cw))
        st_ref[s] = u[(s + 1) * seg - CONV_PAD:(s + 1) * seg]
    o_ref[...] += _dot(jnp.concatenate(vs, axis=0), _cast(wo_ref, wo_bf))


def _conv_sample(x, hist, g, w_in, cw, w_out, *, tn, nseg):
    m, d = x.shape
    n_c, seg = d // tn, m // nseg
    assert n_c * tn == d and seg * nseg == m and seg >= CONV_PAD
    st_spec = pl.BlockSpec((nseg, CONV_PAD, tn), lambda i, c: (0, 0, c))
    return pl.pallas_call(
        functools.partial(_conv_sample_kernel, nseg=nseg, seg=seg),
        grid=(1, n_c),
        in_specs=[
            pl.BlockSpec((m, d), lambda i, c: (0, 0)),
            st_spec,
            pl.BlockSpec((1, d), lambda i, c: (0, 0)),
            pl.BlockSpec((d, tn), lambda i, c: (0, c)),
            pl.BlockSpec((d, tn), lambda i, c: (0, n_c + c)),
            pl.BlockSpec((d, tn), lambda i, c: (0, 2 * n_c + c)),
            pl.BlockSpec((CONV_WIDTH, tn), lambda i, c: (0, c)),
            pl.BlockSpec((tn, d), lambda i, c: (c, 0)),
        ],
        out_specs=[
            pl.BlockSpec((m, d), lambda i, c: (0, 0)),
            st_spec,
            *[pl.BlockSpec((d, tn), lambda i, c: (0, c))] * 3,
            pl.BlockSpec((tn, d), lambda i, c: (c, 0)),
        ],
        out_shape=[
            jax.ShapeDtypeStruct((m, d), F32),
            jax.ShapeDtypeStruct((nseg, CONV_PAD, d), F32),
            *[jax.ShapeDtypeStruct((d, d), BF16)] * 4,
        ],
        scratch_shapes=[pltpu.VMEM((m, d), BF16)],
        compiler_params=_PARAMS,
        name="conv_sample",
    )(x, hist, g, w_in, w_in, w_in, cw, w_out)


def _ffn_sample_kernel(*refs, pool, final, nseg, seg, pos0):
    x_ref, g_ref, w1_ref, w2_ref, fin_ref = refs[:5]
    if pool:
        mg_ref, hist_ref, pw_ref, ps_ref, o_ref, st_ref, w1_bf, w2_bf, pw_bf, z_ref = refs[5:]
    else:
        o_ref, w1_bf, w2_bf, z_ref = refs[5:]
    f = pl.program_id(1)

    @pl.when(f == 0)
    def _():
        x = x_ref[...]
        if pool:
            pw_bf[...] = pw_ref[...].astype(BF16)
            zm = _rmsnorm(x, mg_ref[...])
            frames = (lax.broadcasted_iota(jnp.int32, (seg, 1), 0) + (pos0 + 1)).astype(F32)
            xs = []
            for s in range(nseg):
                rows = slice(s * seg, (s + 1) * seg)
                xs.append(_pool_mix(x[rows], zm[rows], hist_ref[s], frames, pw_bf, ps_ref))
                st_ref[s] = zm[(s + 1) * seg - POOL_PAD:(s + 1) * seg]
            x = jnp.concatenate(xs, axis=0)
        o_ref[...] = x
        z_ref[...] = _rmsnorm(x, g_ref[...]).astype(BF16)

    h = jnp.maximum(_dot(z_ref[...], _cast(w1_ref, w1_bf)), 0.0)
    o_ref[...] += _dot((h * h).astype(BF16), _cast(w2_ref, w2_bf))

    if final:
        @pl.when(f == pl.num_programs(1) - 1)
        def _():
            o_ref[...] = _rmsnorm(o_ref[...], fin_ref[...])


def _ffn_sample(x, g, w1, w2, layer, fin, *, tf, final, nseg, pool_args=None, pos0=0):
    m, d = x.shape
    d_ff = w1.shape[2]
    n_f, seg = d_ff // tf, m // nseg
    assert n_f * tf == d_ff and seg * nseg == m and seg >= POOL_PAD
    pool = pool_args is not None
    row = pl.BlockSpec((1, d), lambda i, f: (0, 0))
    in_specs = [
        pl.BlockSpec((m, d), lambda i, f: (0, 0)),
        row,
        pl.BlockSpec((None, d, tf), lambda i, f: (layer, 0, f)),
        pl.BlockSpec((None, tf, d), lambda i, f: (layer, f, 0)),
        row,
    ]
    out_specs = [pl.BlockSpec((m, d), lambda i, f: (0, 0))]
    out_shape = [jax.ShapeDtypeStruct((m, d), F32)]
    args = [x, g, w1, w2, fin]
    if pool:
        pw_shape = pool_args[2].shape
        st_spec = pl.BlockSpec((nseg, POOL_PAD, d), lambda i, f: (0, 0, 0))
        in_specs += [row, st_spec, pl.BlockSpec(pw_shape, lambda i, f: (0, 0, 0)), row]
        out_specs.append(st_spec)
        out_shape.append(jax.ShapeDtypeStruct((nseg, POOL_PAD, d), F32))
        args += list(pool_args)
    out_specs += [pl.BlockSpec((None, d, tf), lambda i, f: (0, 0, f)),
                  pl.BlockSpec((None, tf, d), lambda i, f: (0, f, 0))]
    out_shape += [jax.ShapeDtypeStruct((1, d, d_ff), BF16),
                  jax.ShapeDtypeStruct((1, d_ff, d), BF16)]
    if pool:
        out_specs.append(pl.BlockSpec(pw_shape, lambda i, f: (0, 0, 0)))
        out_shape.append(jax.ShapeDtypeStruct(pw_shape, BF16))
    return pl.pallas_call(
        functools.partial(_ffn_sample_kernel, pool=pool, final=final, nseg=nseg, seg=seg,
                          pos0=pos0),
        grid=(1, n_f),
        in_specs=in_specs,
        out_specs=out_specs,
        out_shape=out_shape,
        scratch_shapes=[pltpu.VMEM((m, d), BF16)],
        compiler_params=_PARAMS,
        name="pool_ffn_sample" if pool else "ffn_sample",
    )(*args)


def _conv_prompt_kernel(x_ref, hist_ref, g_ref, wb_ref, wc_ref, wh_ref, cw_ref, wo_ref,
                        o_ref, st_ref, z_ref):
    i = pl.program_id(0)
    c = pl.program_id(1)

    @pl.when(c == 0)
    def _():
        x = x_ref[...]
        z_ref[...] = _rmsnorm(x, g_ref[...]).astype(BF16)
        o_ref[...] = x

    @pl.when(i == 0)
    def _():
        st_ref[c] = hist_ref[0]

    z = z_ref[...]
    b = _dot(z, wb_ref[...])
    u = _dot(z, wc_ref[...]) * _dot(z, wh_ref[...])
    v = _conv_gate(b, u, st_ref[c], cw_ref[...])
    st_ref[c] = u[u.shape[0] - CONV_PAD:]
    o_ref[...] += _dot(v, wo_ref[...])


def _conv_prompt(x, hist, g, wb, wc, wh, cw, w_out, *, tm, tn):
    m, d = x.shape
    n_i, n_c = m // tm, d // tn
    assert n_i * tm == m and n_c * tn == d
    w_spec = pl.BlockSpec((d, tn), lambda i, c: (0, c))
    y, st = pl.pallas_call(
        _conv_prompt_kernel,
        grid=(n_i, n_c),
        in_specs=[
            pl.BlockSpec((tm, d), lambda i, c: (i, 0)),
            pl.BlockSpec((1, CONV_PAD, tn), lambda i, c: (0, 0, c)),
            pl.BlockSpec((1, d), lambda i, c: (0, 0)),
            w_spec, w_spec, w_spec,
            pl.BlockSpec((CONV_WIDTH, tn), lambda i, c: (0, c)),
            pl.BlockSpec((tn, d), lambda i, c: (c, 0)),
        ],
        out_specs=[
            pl.BlockSpec((tm, d), lambda i, c: (i, 0)),
            pl.BlockSpec((n_c, CONV_PAD, tn), lambda i, c: (0, 0, 0)),
        ],
        out_shape=[
            jax.ShapeDtypeStruct((m, d), F32),
            jax.ShapeDtypeStruct((n_c, CONV_PAD, tn), F32),
        ],
        scratch_shapes=[pltpu.VMEM((tm, d), BF16)],
        compiler_params=_PARAMS,
        name="conv_prompt",
    )(x, hist, g, wb, wc, wh, cw, w_out)
    return y, st.transpose(1, 0, 2).reshape(1, CONV_PAD, d)


def _ffn_prompt_kernel(*refs, pool, final, n_i, tm, rb):
    x_hbm, g_ref, w1_ref, w2_ref, fin_ref = refs[:5]
    if pool:
        mg_ref, hist_ref, pw_ref, ps_ref, o_ref, st_ref = refs[5:11]
    else:
        o_ref = refs[5]
    z_ref, xbuf_ref, sem = refs[-3:]
    i = pl.program_id(0)
    f = pl.program_id(1)
    n_f = pl.num_programs(1)

    def x_copy(tile):
        return pltpu.make_async_copy(x_hbm.at[pl.ds(tile * tm, tm)], xbuf_ref, sem)

    @pl.when((i == 0) & (f == 0))
    def _():
        x_copy(0).start()
        if pool:
            st_ref[...] = hist_ref[...]

    @pl.when(f == 0)
    def _():
        x_copy(i).wait()
        t = lax.broadcasted_iota(jnp.int32, (rb, 1), 0)
        for k in range(tm // rb):
            rows = slice(k * rb, (k + 1) * rb)
            x = xbuf_ref[rows]
            if pool:
                zm = _rmsnorm(x, mg_ref[...])
                frames = (t + (i * tm + (k * rb + 1))).astype(F32)
                x = _pool_mix(x, zm, st_ref[0], frames, pw_ref, ps_ref)
                st_ref[0] = zm[rb - POOL_PAD:]
            o_ref[rows] = x
            z_ref[rows] = _rmsnorm(x, g_ref[...]).astype(BF16)

    @pl.when((f == 1) & (i + 1 < n_i))
    def _():
        x_copy(i + 1).start()

    h = jnp.maximum(_dot(z_ref[...], w1_ref[...]), 0.0)
    o_ref[...] += _dot((h * h).astype(BF16), w2_ref[...])

    if final:
        @pl.when(f == n_f - 1)
        def _():
            o_ref[...] = _rmsnorm(o_ref[...], fin_ref[...])


def _ffn_prompt(x, g, w1, w2, fin, *, tm, tf, rb, final, pool_args=None):
    m, d = x.shape
    d_ff = w1.shape[2]
    n_i, n_f = m // tm, d_ff // tf
    assert n_i * tm == m and n_f * tf == d_ff and n_f >= 2
    assert tm % rb == 0 and rb % SUBLANES == 0 and rb >= POOL_PAD
    pool = pool_args is not None
    row = pl.BlockSpec((1, d), lambda i, f: (0, 0))
    in_specs = [
        pl.BlockSpec(memory_space=pl.ANY),
        row,
        pl.BlockSpec((None, d, tf), lambda i, f: (0, 0, f)),
        pl.BlockSpec((None, tf, d), lambda i, f: (0, f, 0)),
        row,
    ]
    out_specs = [pl.BlockSpec((tm, d), lambda i, f: (i, 0))]
    out_shape = [jax.ShapeDtypeStruct((m, d), F32)]
    args = [x, g, w1, w2, fin]
    if pool:
        st_spec = pl.BlockSpec((1, POOL_PAD, d), lambda i, f: (0, 0, 0))
        in_specs += [row, st_spec,
                     pl.BlockSpec(pool_args[2].shape, lambda i, f: (0, 0, 0), pipeline_mode=_ONCE),
                     row]
        out_specs.append(st_spec)
        out_shape.append(jax.ShapeDtypeStruct((1, POOL_PAD, d), F32))
        args += list(pool_args)
    return pl.pallas_call(
        functools.partial(_ffn_prompt_kernel, pool=pool, final=final, n_i=n_i, tm=tm, rb=rb),
        grid=(n_i, n_f),
        in_specs=in_specs,
        out_specs=out_specs,
        out_shape=out_shape,
        scratch_shapes=[pltpu.VMEM((tm, d), BF16), pltpu.VMEM((tm, d), F32),
                        pltpu.SemaphoreType.DMA(())],
        compiler_params=_PARAMS,
        name="pool_ffn_prompt" if pool else "ffn_prompt",
    )(*args)


def _pad_hist(hist, pad):
    b, h, d = hist.shape
    return jnp.concatenate([jnp.zeros((b, pad - h, d), hist.dtype), hist], axis=1)


def _states(conv_st, pool_st):
    return (conv_st[None, :, CONV_PAD - CONV_HIST:, :], pool_st[None, :, POOL_PAD - POOL_HIST:, :])


def kernel(x_prompt, x_sample, cache_conv, cache_pool, mix_norm, ffn_norm, conv_w_in, conv_w,
           conv_w_out, pool_w, pool_scale, ffn_w1, ffn_w2, final_norm):
    depth, d = mix_norm.shape
    assert depth == 2 and conv_w_in.shape[0] == 1 and pool_w.shape[0] == 1
    bp, lp, _ = x_prompt.shape
    bs, ls, _ = x_sample.shape
    assert bp == 1
    mix_g0, mix_g1 = mix_norm[0:1], mix_norm[1:2]
    ffn_g0, ffn_g1 = ffn_norm[0:1], ffn_norm[1:2]
    fin = final_norm.reshape(1, d)
    ps = pool_scale[0:1]

    x1, conv_s, wb, wc, wh, wo = _conv_sample(
        x_sample.reshape(bs * ls, d), _pad_hist(cache_conv[0], CONV_PAD), mix_g0, conv_w_in[0],
        conv_w[0], conv_w_out[0], tn=256, nseg=bs)
    x2, w1_0, w2_0 = _ffn_sample(x1, ffn_g0, ffn_w1, ffn_w2, 0, fin, tf=512, final=False, nseg=bs)
    y_s, pool_s, w1_1, w2_1, pw = _ffn_sample(
        x2, ffn_g1, ffn_w1, ffn_w2, 1, fin, tf=512, final=True, nseg=bs,
        pool_args=(mix_g1, _pad_hist(cache_pool[0], POOL_PAD), pool_w[0], ps), pos0=PAST_LEN)

    x1, conv_p = _conv_prompt(x_prompt.reshape(lp, d), jnp.zeros((1, CONV_PAD, d), F32), mix_g0,
                              wb, wc, wh, conv_w[0], wo, tm=512, tn=512)
    ffn_tiles = dict(tm=1024, tf=1024, rb=256)
    x2, = _ffn_prompt(x1, ffn_g0, w1_0, w2_0, fin, final=False, **ffn_tiles)
    y_p, pool_p = _ffn_prompt(x2, ffn_g1, w1_1, w2_1, fin, final=True, **ffn_tiles,
                              pool_args=(mix_g1, jnp.zeros((1, POOL_PAD, d), F32), pw, ps))

    conv_state_p, pool_state_p = _states(conv_p, pool_p)
    conv_state_s, pool_state_s = _states(conv_s, pool_s)
    return (y_p.reshape(bp, lp, d), y_s.reshape(bs, ls, d), conv_state_p, pool_state_p,
            conv_state_s, pool_state_s)
```

```python
import functools

import jax
import jax.numpy as jnp
from jax import lax
from jax.experimental import pallas as pl
from jax.experimental.pallas import tpu as pltpu

EPS = 1e-6
CONV_WIDTH = 3
CONV_HIST = CONV_WIDTH - 1
POOL_WINDOWS = (2, 4, 8, 16)
POOL_HIST = max(POOL_WINDOWS) - 1
PAST_LEN = 1024
SUBLANES = 8
CONV_PAD = SUBLANES
POOL_PAD = 2 * SUBLANES
VMEM_LIMIT = 56 * 1024 * 1024

BF16 = jnp.bfloat16
F32 = jnp.float32
_PARAMS = pltpu.CompilerParams(dimension_semantics=("arbitrary", "arbitrary"),
                               vmem_limit_bytes=VMEM_LIMIT)
_ONCE = pl.Buffered(1)


def _rmsnorm(x, g):
    r = lax.rsqrt(jnp.mean(x * x, axis=-1, keepdims=True) + EPS)
    return (x * r) * g


def _dot(a, b):
    return jnp.dot(a, b, preferred_element_type=F32)


def _conv_gate(b, u, hist, cw):
    ext = jnp.concatenate([hist, u], axis=0)
    p1 = pltpu.roll(ext, 1, 0)[CONV_PAD:]
    p2 = pltpu.roll(ext, 2, 0)[CONV_PAD:]
    conv = p2 * cw[0:1] + p1 * cw[1:2] + u * cw[2:3]
    return (b * conv).astype(BF16)


def _pool_mix(x, z, hist, frames, pw_ref, ps_ref):
    r, d = x.shape
    gc = d // len(POOL_WINDOWS)
    outs = []
    for gi, w in enumerate(POOL_WINDOWS):
        cols = slice(gi * gc, (gi + 1) * gc)
        zs = z[:, cols]
        a = jnp.concatenate([hist[:, cols], zs], axis=0)
        shift = 1
        while shift < min(w, SUBLANES):
            a = a + pltpu.roll(a, shift, 0)
            shift *= 2
        a = a[SUBLANES:]
        a = a[SUBLANES:] + a[:r] if w > SUBLANES else a[SUBLANES:]
        pooled = a * (1.0 / jnp.minimum(frames, float(w))) - zs
        y = _dot(pooled.astype(BF16), pw_ref[gi]) * ps_ref[:, cols]
        outs.append(x[:, cols] + y)
    return jnp.concatenate(outs, axis=1)


def _cast(w_ref, w_bf_ref):
    w_bf_ref[...] = w_ref[...].astype(BF16)
    return w_bf_ref[...]


def _conv_sample_kernel(x_ref, hist_ref, g_ref, wb_ref, wc_ref, wh_ref, cw_ref, wo_ref,
                        o_ref, st_ref, wb_bf, wc_bf, wh_bf, wo_bf, z_ref, *, nseg, seg):
    c = pl.program_id(1)

    @pl.when(c == 0)
    def _():
        x = x_ref[...]
        z_ref[...] = _rmsnorm(x, g_ref[...]).astype(BF16)
        o_ref[...] = x

    z = z_ref[...]
    b = _dot(z, _cast(wb_ref, wb_bf))
    u = _dot(z, _cast(wc_ref, wc_bf)) * _dot(z, _cast(wh_ref, wh_bf))
    cw = cw_ref[...]
    vs = []
    for s in range(nseg):
        rows = slice(s * seg, (s + 1) * seg)
        vs.append(_conv_gate(b[rows], u[rows], hist_ref[s], cw))
        st_ref[s] = u[(s + 1) * seg - CONV_PAD:(s + 1) * seg]
    o_ref[...] += _dot(jnp.concatenate(vs, axis=0), _cast(wo_ref, wo_bf))


def _conv_sample(x, hist, g, w_in, cw, w_out, *, tn, nseg):
    m, d = x.shape
    n_c, seg = d // tn, m // nseg
    assert n_c * tn == d and seg * nseg == m and seg >= CONV_PAD
    st_spec = pl.BlockSpec((nseg, CONV_PAD, tn), lambda i, c: (0, 0, c))
    return pl.pallas_call(
        functools.partial(_conv_sample_kernel, nseg=nseg, seg=seg),
        grid=(1, n_c),
        in_specs=[
            pl.BlockSpec((m, d), lambda i, c: (0, 0)),
            st_spec,
            pl.BlockSpec((1, d), lambda i, c: (0, 0)),
            pl.BlockSpec((d, tn), lambda i, c: (0, c)),
            pl.BlockSpec((d, tn), lambda i, c: (0, n_c + c)),
            pl.BlockSpec((d, tn), lambda i, c: (0, 2 * n_c + c)),
            pl.BlockSpec((CONV_WIDTH, tn), lambda i, c: (0, c)),
            pl.BlockSpec((tn, d), lambda i, c: (c, 0)),
        ],
        out_specs=[
            pl.BlockSpec((m, d), lambda i, c: (0, 0)),
            st_spec,
            *[pl.BlockSpec((d, tn), lambda i, c: (0, c))] * 3,
            pl.BlockSpec((tn, d), lambda i, c: (c, 0)),
        ],
        out_shape=[
            jax.ShapeDtypeStruct((m, d), F32),
            jax.ShapeDtypeStruct((nseg, CONV_PAD, d), F32),
            *[jax.ShapeDtypeStruct((d, d), BF16)] * 4,
        ],
        scratch_shapes=[pltpu.VMEM((m, d), BF16)],
        compiler_params=_PARAMS,
        name="conv_sample",
    )(x, hist, g, w_in, w_in, w_in, cw, w_out)


def _ffn_sample_kernel(*refs, pool, final, nseg, seg, pos0):
    x_ref, g_ref, w1_ref, w2_ref, fin_ref = refs[:5]
    if pool:
        mg_ref, hist_ref, pw_ref, ps_ref, o_ref, st_ref, pw_bf, z_ref = refs[5:]
    else:
        o_ref, z_ref = refs[5:]
    f = pl.program_id(1)

    @pl.when(f == 0)
    def _():
        x = x_ref[...]
        if pool:
            pw_bf[...] = pw_ref[...].astype(BF16)
            zm = _rmsnorm(x, mg_ref[...])
            frames = (lax.broadcasted_iota(jnp.int32, (seg, 1), 0) + (pos0 + 1)).astype(F32)
            xs = []
            for s in range(nseg):
                rows = slice(s * seg, (s + 1) * seg)
                xs.append(_pool_mix(x[rows], zm[rows], hist_ref[s], frames, pw_bf, ps_ref))
                st_ref[s] = zm[(s + 1) * seg - POOL_PAD:(s + 1) * seg]
            x = jnp.concatenate(xs, axis=0)
        o_ref[...] = x
        z_ref[...] = _rmsnorm(x, g_ref[...]).astype(BF16)

    h = jnp.maximum(_dot(z_ref[...], w1_ref[...]), 0.0)
    o_ref[...] += _dot((h * h).astype(BF16), w2_ref[...])

    if final:
        @pl.when(f == pl.num_programs(1) - 1)
        def _():
            o_ref[...] = _rmsnorm(o_ref[...], fin_ref[...])


def _ffn_sample(x, g, w1, w2, fin, *, tf, final, nseg, pool_args=None, pos0=0):
    m, d = x.shape
    d_ff = w1.shape[2]
    n_f, seg = d_ff // tf, m // nseg
    assert n_f * tf == d_ff and seg * nseg == m and seg >= POOL_PAD
    pool = pool_args is not None
    row = pl.BlockSpec((1, d), lambda i, f: (0, 0))
    in_specs = [
        pl.BlockSpec((m, d), lambda i, f: (0, 0)),
        row,
        pl.BlockSpec((None, d, tf), lambda i, f: (0, 0, f)),
        pl.BlockSpec((None, tf, d), lambda i, f: (0, f, 0)),
        row,
    ]
    out_specs = [pl.BlockSpec((m, d), lambda i, f: (0, 0))]
    out_shape = [jax.ShapeDtypeStruct((m, d), F32)]
    args = [x, g, w1, w2, fin]
    if pool:
        pw_shape = pool_args[2].shape
        st_spec = pl.BlockSpec((nseg, POOL_PAD, d), lambda i, f: (0, 0, 0))
        in_specs += [row, st_spec, pl.BlockSpec(pw_shape, lambda i, f: (0, 0, 0)), row]
        out_specs.append(st_spec)
        out_shape.append(jax.ShapeDtypeStruct((nseg, POOL_PAD, d), F32))
        args += list(pool_args)
        out_specs.append(pl.BlockSpec(pw_shape, lambda i, f: (0, 0, 0)))
        out_shape.append(jax.ShapeDtypeStruct(pw_shape, BF16))
    return pl.pallas_call(
        functools.partial(_ffn_sample_kernel, pool=pool, final=final, nseg=nseg, seg=seg,
                          pos0=pos0),
        grid=(1, n_f),
        in_specs=in_specs,
        out_specs=out_specs,
        out_shape=out_shape,
        scratch_shapes=[pltpu.VMEM((m, d), BF16)],
        compiler_params=_PARAMS,
        name="pool_ffn_sample" if pool else "ffn_sample",
    )(*args)


def _fetch_row_tile(x_hbm, xbuf_ref, sem, tm, n_i):
    i = pl.program_id(0)
    k = pl.program_id(1)

    def copy(tile):
        return pltpu.make_async_copy(x_hbm.at[pl.ds(tile * tm, tm)], xbuf_ref, sem)

    @pl.when((i == 0) & (k == 0))
    def _():
        copy(0).start()

    @pl.when(k == 0)
    def _():
        copy(i).wait()

    @pl.when((k == 1) & (i + 1 < n_i))
    def _():
        copy(i + 1).start()


def _slab_specs(w, layer, n_steps, n_k):
    _, r, c = w.shape
    rs = r // n_steps
    assert rs * n_steps == r and rs % (2 * SUBLANES) == 0
    return (pl.BlockSpec((None, rs, c), lambda i, k: (layer, i * n_k + k, 0)),
            pl.BlockSpec((None, rs, c), lambda i, k: (0, i * n_k + k, 0)),
            jax.ShapeDtypeStruct((1, r, c), BF16))


def _conv_prompt_kernel(x_hbm, hist_ref, g_ref, wb_ref, wc_ref, wh_ref, cw_ref, wo_ref, n1_ref,
                        n2_ref, o_ref, st_ref, n1_bf, n2_bf, z_ref, xbuf_ref, sem, *,
                        n_i, tm, rb):
    i = pl.program_id(0)
    c = pl.program_id(1)
    _fetch_row_tile(x_hbm, xbuf_ref, sem, tm, n_i)
    n1_bf[...] = n1_ref[...].astype(BF16)
    n2_bf[...] = n2_ref[...].astype(BF16)

    @pl.when(c == 0)
    def _():
        for k in range(tm // rb):
            rows = slice(k * rb, (k + 1) * rb)
            z_ref[rows] = _rmsnorm(xbuf_ref[rows], g_ref[...]).astype(BF16)

    @pl.when(i == 0)
    def _():
        st_ref[c] = hist_ref[0]

    def step(base_ref):
        z = z_ref[...]
        b = _dot(z, wb_ref[...])
        u = _dot(z, wc_ref[...]) * _dot(z, wh_ref[...])
        v = _conv_gate(b, u, st_ref[c], cw_ref[...])
        st_ref[c] = u[u.shape[0] - CONV_PAD:]
        o_ref[...] = base_ref[...] + _dot(v, wo_ref[...])

    pl.when(c == 0)(lambda: step(xbuf_ref))
    pl.when(c > 0)(lambda: step(o_ref))


def _conv_prompt(x, hist, g, wb, wc, wh, cw, w_out, nxt, *, tm, tn, rb):
    m, d = x.shape
    n_i, n_c = m // tm, d // tn
    assert n_i * tm == m and n_c * tn == d and n_c >= 2 and tm % rb == 0 and rb % SUBLANES == 0
    w_spec = pl.BlockSpec((d, tn), lambda i, c: (0, c))
    n1_in, n1_out, n1_shape = _slab_specs(nxt[0], nxt[2], n_i * n_c, n_c)
    n2_in, n2_out, n2_shape = _slab_specs(nxt[1], nxt[2], n_i * n_c, n_c)
    y, st, n1_bf, n2_bf = pl.pallas_call(
        functools.partial(_conv_prompt_kernel, n_i=n_i, tm=tm, rb=rb),
        grid=(n_i, n_c),
        in_specs=[
            pl.BlockSpec(memory_space=pl.ANY),
            pl.BlockSpec((1, CONV_PAD, tn), lambda i, c: (0, 0, c)),
            pl.BlockSpec((1, d), lambda i, c: (0, 0)),
            w_spec, w_spec, w_spec,
            pl.BlockSpec((CONV_WIDTH, tn), lambda i, c: (0, c)),
            pl.BlockSpec((tn, d), lambda i, c: (c, 0)),
            n1_in, n2_in,
        ],
        out_specs=[
            pl.BlockSpec((tm, d), lambda i, c: (i, 0)),
            pl.BlockSpec((n_c, CONV_PAD, tn), lambda i, c: (0, 0, 0)),
            n1_out, n2_out,
        ],
        out_shape=[
            jax.ShapeDtypeStruct((m, d), F32),
            jax.ShapeDtypeStruct((n_c, CONV_PAD, tn), F32),
            n1_shape, n2_shape,
        ],
        scratch_shapes=[pltpu.VMEM((tm, d), BF16), pltpu.VMEM((tm, d), F32),
                        pltpu.SemaphoreType.DMA(())],
        compiler_params=_PARAMS,
        name="conv_prompt",
    )(x, hist, g, wb, wc, wh, cw, w_out, nxt[0], nxt[1])
    return y, st.transpose(1, 0, 2).reshape(1, CONV_PAD, d), n1_bf, n2_bf


def _ffn_prompt_kernel(*refs, pool, final, n_i, tm, rb):
    x_hbm, g_ref, w1_ref, w2_ref, fin_ref = refs[:5]
    if pool:
        mg_ref, hist_ref, pw_ref, ps_ref, o_ref, st_ref = refs[5:11]
    else:
        n1_ref, n2_ref, o_ref, n1_bf, n2_bf = refs[5:10]
    z_ref, xbuf_ref, sem = refs[-3:]
    i = pl.program_id(0)
    f = pl.program_id(1)
    n_f = pl.num_programs(1)
    _fetch_row_tile(x_hbm, xbuf_ref, sem, tm, n_i)
    if not pool:
        n1_bf[...] = n1_ref[...].astype(BF16)
        n2_bf[...] = n2_ref[...].astype(BF16)

    if pool:
        @pl.when((i == 0) & (f == 0))
        def _():
            st_ref[...] = hist_ref[...]

    @pl.when(f == 0)
    def _():
        t = lax.broadcasted_iota(jnp.int32, (rb, 1), 0)
        for k in range(tm // rb):
            rows = slice(k * rb, (k + 1) * rb)
            x = xbuf_ref[rows]
            if pool:
                zm = _rmsnorm(x, mg_ref[...])
                frames = (t + (i * tm + (k * rb + 1))).astype(F32)
                x = _pool_mix(x, zm, st_ref[0], frames, pw_ref, ps_ref)
                st_ref[0] = zm[rb - POOL_PAD:]
                o_ref[rows] = x
            z_ref[rows] = _rmsnorm(x, g_ref[...]).astype(BF16)

    def step(base_ref):
        h = jnp.maximum(_dot(z_ref[...], w1_ref[...]), 0.0)
        o_ref[...] = base_ref[...] + _dot((h * h).astype(BF16), w2_ref[...])

    if pool:
        step(o_ref)
    else:
        pl.when(f == 0)(lambda: step(xbuf_ref))
        pl.when(f > 0)(lambda: step(o_ref))

    if final:
        @pl.when(f == n_f - 1)
        def _():
            o_ref[...] = _rmsnorm(o_ref[...], fin_ref[...])


def _ffn_prompt(x, g, w1, w2, fin, *, tm, tf, rb, final, pool_args=None, nxt=None):
    m, d = x.shape
    d_ff = w1.shape[2]
    n_i, n_f = m // tm, d_ff // tf
    assert n_i * tm == m and n_f * tf == d_ff and n_f >= 2
    assert tm % rb == 0 and rb % SUBLANES == 0 and rb >= POOL_PAD
    pool = pool_args is not None
    assert pool != (nxt is not None)
    row = pl.BlockSpec((1, d), lambda i, f: (0, 0))
    in_specs = [
        pl.BlockSpec(memory_space=pl.ANY),
        row,
        pl.BlockSpec((None, d, tf), lambda i, f: (0, 0, f)),
        pl.BlockSpec((None, tf, d), lambda i, f: (0, f, 0)),
        row,
    ]
    out_specs = [pl.BlockSpec((tm, d), lambda i, f: (i, 0))]
    out_shape = [jax.ShapeDtypeStruct((m, d), F32)]
    args = [x, g, w1, w2, fin]
    if pool:
        st_spec = pl.BlockSpec((1, POOL_PAD, d), lambda i, f: (0, 0, 0))
        in_specs += [row, st_spec,
                     pl.BlockSpec(pool_args[2].shape, lambda i, f: (0, 0, 0), pipeline_mode=_ONCE),
                     row]
        out_specs.append(st_spec)
        out_shape.append(jax.ShapeDtypeStruct((1, POOL_PAD, d), F32))
        args += list(pool_args)
    else:
        n1_in, n1_out, n1_shape = _slab_specs(nxt[0], nxt[2], n_i * n_f, n_f)
        n2_in, n2_out, n2_shape = _slab_specs(nxt[1], nxt[2], n_i * n_f, n_f)
        in_specs += [n1_in, n2_in]
        out_specs += [n1_out, n2_out]
        out_shape += [n1_shape, n2_shape]
        args += [nxt[0], nxt[1]]
    return pl.pallas_call(
        functools.partial(_ffn_prompt_kernel, pool=pool, final=final, n_i=n_i, tm=tm, rb=rb),
        grid=(n_i, n_f),
        in_specs=in_specs,
        out_specs=out_specs,
        out_shape=out_shape,
        scratch_shapes=[pltpu.VMEM((tm, d), BF16), pltpu.VMEM((tm, d), F32),
                        pltpu.SemaphoreType.DMA(())],
        compiler_params=_PARAMS,
        name="pool_ffn_prompt" if pool else "ffn_prompt",
    )(*args)


def _pad_hist(hist, pad):
    b, h, d = hist.shape
    return jnp.concatenate([jnp.zeros((b, pad - h, d), hist.dtype), hist], axis=1)


def _states(conv_st, pool_st):
    return (conv_st[None, :, CONV_PAD - CONV_HIST:, :], pool_st[None, :, POOL_PAD - POOL_HIST:, :])


def kernel(x_prompt, x_sample, cache_conv, cache_pool, mix_norm, ffn_norm, conv_w_in, conv_w,
           conv_w_out, pool_w, pool_scale, ffn_w1, ffn_w2, final_norm):
    depth, d = mix_norm.shape
    assert depth == 2 and conv_w_in.shape[0] == 1 and pool_w.shape[0] == 1
    bp, lp, _ = x_prompt.shape
    bs, ls, _ = x_sample.shape
    assert bp == 1
    mix_g0, mix_g1 = mix_norm[0:1], mix_norm[1:2]
    ffn_g0, ffn_g1 = ffn_norm[0:1], ffn_norm[1:2]
    fin = final_norm.reshape(1, d)
    ps = pool_scale[0:1]

    xs = x_sample.reshape(bs * ls, d)
    xp = x_prompt.reshape(lp, d)
    ffn_tiles = dict(tm=1024, tf=1024, rb=256)

    xs, conv_s, wb, wc, wh, wo = _conv_sample(
        xs, _pad_hist(cache_conv[0], CONV_PAD), mix_g0, conv_w_in[0], conv_w[0], conv_w_out[0],
        tn=256, nseg=bs)
    xp, conv_p, w1_0, w2_0 = _conv_prompt(
        xp, jnp.zeros((1, CONV_PAD, d), F32), mix_g0, wb, wc, wh, conv_w[0], wo,
        (ffn_w1, ffn_w2, 0), tm=512, tn=512, rb=256)
    xs, = _ffn_sample(xs, ffn_g0, w1_0, w2_0, fin, tf=1024, final=False, nseg=bs)
    xp, w1_1, w2_1 = _ffn_prompt(xp, ffn_g0, w1_0, w2_0, fin, final=False, **ffn_tiles,
                                 nxt=(ffn_w1, ffn_w2, 1))
    y_s, pool_s, pw = _ffn_sample(
        xs, ffn_g1, w1_1, w2_1, fin, tf=1024, final=True, nseg=bs,
        pool_args=(mix_g1, _pad_hist(cache_pool[0], POOL_PAD), pool_w[0], ps), pos0=PAST_LEN)
    y_p, pool_p = _ffn_prompt(xp, ffn_g1, w1_1, w2_1, fin, final=True, **ffn_tiles,
                              pool_args=(mix_g1, jnp.zeros((1, POOL_PAD, d), F32), pw, ps))

    conv_state_p, pool_state_p = _states(conv_p, pool_p)
    conv_state_s, pool_state_s = _states(conv_s, pool_s)
    return (y_p.reshape(bp, lp, d), y_s.reshape(bs, ls, d), conv_state_p, pool_state_p,
            conv_state_s, pool_state_s)
```

```python
import functools

import jax
import jax.numpy as jnp
from jax import lax
from jax.experimental import pallas as pl
from jax.experimental.pallas import tpu as pltpu

EPS = 1e-6
CONV_WIDTH = 3
CONV_HIST = CONV_WIDTH - 1
POOL_WINDOWS = (2, 4, 8, 16)
POOL_HIST = max(POOL_WINDOWS) - 1
PAST_LEN = 1024
SUBLANES = 8
CONV_PAD = SUBLANES
POOL_PAD = 2 * SUBLANES
VMEM_LIMIT = 56 * 1024 * 1024

BF16 = jnp.bfloat16
F32 = jnp.float32
_PARAMS = pltpu.CompilerParams(dimension_semantics=("arbitrary", "arbitrary"),
                               vmem_limit_bytes=VMEM_LIMIT)
_ONCE = pl.Buffered(1)


def _rmsnorm(x, g):
    r = lax.rsqrt(jnp.mean(x * x, axis=-1, keepdims=True) + EPS)
    return (x * r) * g


def _dot(a, b):
    return jnp.dot(a, b, preferred_element_type=F32)


def _conv_gate(b, u, hist, cw):
    ext = jnp.concatenate([hist, u], axis=0)
    p1 = pltpu.roll(ext, 1, 0)[CONV_PAD:]
    p2 = pltpu.roll(ext, 2, 0)[CONV_PAD:]
    conv = p2 * cw[0:1] + p1 * cw[1:2] + u * cw[2:3]
    return (b * conv).astype(BF16)


def _pool_mix(x, z, hist, frames, pw_ref, ps_ref):
    r, d = x.shape
    gc = d // len(POOL_WINDOWS)
    outs = []
    for gi, w in enumerate(POOL_WINDOWS):
        cols = slice(gi * gc, (gi + 1) * gc)
        zs = z[:, cols]
        a = jnp.concatenate([hist[:, cols], zs], axis=0)
        shift = 1
        while shift < min(w, SUBLANES):
            a = a + pltpu.roll(a, shift, 0)
            shift *= 2
        a = a[SUBLANES:]
        a = a[SUBLANES:] + a[:r] if w > SUBLANES else a[SUBLANES:]
        pooled = a * (1.0 / jnp.minimum(frames, float(w))) - zs
        y = _dot(pooled.astype(BF16), pw_ref[gi]) * ps_ref[:, cols]
        outs.append(x[:, cols] + y)
    return jnp.concatenate(outs, axis=1)


def _cast(w_ref, w_bf_ref):
    w_bf_ref[...] = w_ref[...].astype(BF16)
    return w_bf_ref[...]


def _conv_sample_kernel(x_ref, hist_ref, g_ref, wb_ref, wc_ref, wh_ref, cw_ref, wo_ref,
                        o_ref, st_ref, wb_bf, wc_bf, wh_bf, wo_bf, z_ref, *, nseg, seg):
    c = pl.program_id(1)

    @pl.when(c == 0)
    def _():
        x = x_ref[...]
        z_ref[...] = _rmsnorm(x, g_ref[...]).astype(BF16)
        o_ref[...] = x

    z = z_ref[...]
    b = _dot(z, _cast(wb_ref, wb_bf))
    u = _dot(z, _cast(wc_ref, wc_bf)) * _dot(z, _cast(wh_ref, wh_bf))
    cw = cw_ref[...]
    vs = []
    for s in range(nseg):
        rows = slice(s * seg, (s + 1) * seg)
        vs.append(_conv_gate(b[rows], u[rows], hist_ref[s], cw))
        st_ref[s] = u[(s + 1) * seg - CONV_PAD:(s + 1) * seg]
    o_ref[...] += _dot(jnp.concatenate(vs, axis=0), _cast(wo_ref, wo_bf))


def _conv_sample(x, hist, g, w_in, cw, w_out, *, tn, nseg):
    m, d = x.shape
    n_c, seg = d // tn, m // nseg
    assert n_c * tn == d and seg * nseg == m and seg >= CONV_PAD
    st_spec = pl.BlockSpec((nseg, CONV_PAD, tn), lambda i, c: (0, 0, c))
    return pl.pallas_call(
        functools.partial(_conv_sample_kernel, nseg=nseg, seg=seg),
        grid=(1, n_c),
        in_specs=[
            pl.BlockSpec((m, d), lambda i, c: (0, 0)),
            st_spec,
            pl.BlockSpec((1, d), lambda i, c: (0, 0)),
            pl.BlockSpec((d, tn), lambda i, c: (0, c)),
            pl.BlockSpec((d, tn), lambda i, c: (0, n_c + c)),
            pl.BlockSpec((d, tn), lambda i, c: (0, 2 * n_c + c)),
            pl.BlockSpec((CONV_WIDTH, tn), lambda i, c: (0, c)),
            pl.BlockSpec((tn, d), lambda i, c: (c, 0)),
        ],
        out_specs=[
            pl.BlockSpec((m, d), lambda i, c: (0, 0)),
            st_spec,
            *[pl.BlockSpec((d, tn), lambda i, c: (0, c))] * 3,
            pl.BlockSpec((tn, d), lambda i, c: (c, 0)),
        ],
        out_shape=[
            jax.ShapeDtypeStruct((m, d), F32),
            jax.ShapeDtypeStruct((nseg, CONV_PAD, d), F32),
            *[jax.ShapeDtypeStruct((d, d), BF16)] * 4,
        ],
        scratch_shapes=[pltpu.VMEM((m, d), BF16)],
        compiler_params=_PARAMS,
        name="conv_sample",
    )(x, hist, g, w_in, w_in, w_in, cw, w_out)


def _ffn_sample_kernel(*refs, pool, final, nseg, seg, pos0):
    x_ref, g_ref, w1_ref, w2_ref, fin_ref = refs[:5]
    if pool:
        mg_ref, hist_ref, pw_ref, ps_ref, o_ref, st_ref, pw_bf, z_ref = refs[5:]
    else:
        o_ref, z_ref = refs[5:]
    f = pl.program_id(1)

    @pl.when(f == 0)
    def _():
        x = x_ref[...]
        if pool:
            pw_bf[...] = pw_ref[...].astype(BF16)
            zm = _rmsnorm(x, mg_ref[...])
            frames = (lax.broadcasted_iota(jnp.int32, (seg, 1), 0) + (pos0 + 1)).astype(F32)
            xs = []
            for s in range(nseg):
                rows = slice(s * seg, (s + 1) * seg)
                xs.append(_pool_mix(x[rows], zm[rows], hist_ref[s], frames, pw_bf, ps_ref))
                st_ref[s] = zm[(s + 1) * seg - POOL_PAD:(s + 1) * seg]
            x = jnp.concatenate(xs, axis=0)
        o_ref[...] = x
        z_ref[...] = _rmsnorm(x, g_ref[...]).astype(BF16)

    h = jnp.maximum(_dot(z_ref[...], w1_ref[...]), 0.0)
    o_ref[...] += _dot((h * h).astype(BF16), w2_ref[...])

    if final:
        @pl.when(f == pl.num_programs(1) - 1)
        def _():
            o_ref[...] = _rmsnorm(o_ref[...], fin_ref[...])


def _ffn_sample(x, g, w1, w2, fin, *, tf, final, nseg, pool_args=None, pos0=0):
    m, d = x.shape
    d_ff = w1.shape[2]
    n_f, seg = d_ff // tf, m // nseg
    assert n_f * tf == d_ff and seg * nseg == m and seg >= POOL_PAD
    pool = pool_args is not None
    row = pl.BlockSpec((1, d), lambda i, f: (0, 0))
    in_specs = [
        pl.BlockSpec((m, d), lambda i, f: (0, 0)),
        row,
        pl.BlockSpec((None, d, tf), lambda i, f: (0, 0, f)),
        pl.BlockSpec((None, tf, d), lambda i, f: (0, f, 0)),
        row,
    ]
    out_specs = [pl.BlockSpec((m, d), lambda i, f: (0, 0))]
    out_shape = [jax.ShapeDtypeStruct((m, d), F32)]
    args = [x, g, w1, w2, fin]
    if pool:
        pw_shape = pool_args[2].shape
        st_spec = pl.BlockSpec((nseg, POOL_PAD, d), lambda i, f: (0, 0, 0))
        in_specs += [row, st_spec, pl.BlockSpec(pw_shape, lambda i, f: (0, 0, 0)), row]
        out_specs.append(st_spec)
        out_shape.append(jax.ShapeDtypeStruct((nseg, POOL_PAD, d), F32))
        args += list(pool_args)
        out_specs.append(pl.BlockSpec(pw_shape, lambda i, f: (0, 0, 0)))
        out_shape.append(jax.ShapeDtypeStruct(pw_shape, BF16))
    return pl.pallas_call(
        functools.partial(_ffn_sample_kernel, pool=pool, final=final, nseg=nseg, seg=seg,
                          pos0=pos0),
        grid=(1, n_f),
        in_specs=in_specs,
        out_specs=out_specs,
        out_shape=out_shape,
        scratch_shapes=[pltpu.VMEM((m, d), BF16)],
        compiler_params=_PARAMS,
        name="pool_ffn_sample" if pool else "ffn_sample",
    )(*args)


def _fetch_row_tile(x_hbm, xbuf_ref, sem, tm, n_i):
    i = pl.program_id(0)
    k = pl.program_id(1)

    def copy(tile):
        return pltpu.make_async_copy(x_hbm.at[pl.ds(tile * tm, tm)], xbuf_ref, sem)

    @pl.when((i == 0) & (k == 0))
    def _():
        copy(0).start()

    @pl.when(k == 0)
    def _():
        copy(i).wait()

    @pl.when((k == 1) & (i + 1 < n_i))
    def _():
        copy(i + 1).start()


def _slab_specs(w, layer, n_steps, n_k):
    _, r, c = w.shape
    rs = r // n_steps
    assert rs * n_steps == r and rs % (2 * SUBLANES) == 0
    return (pl.BlockSpec((None, rs, c), lambda i, k: (layer, i * n_k + k, 0)),
            pl.BlockSpec((None, rs, c), lambda i, k: (0, i * n_k + k, 0)),
            jax.ShapeDtypeStruct((1, r, c), BF16))


def _conv_prompt_kernel(x_hbm, hist_ref, g_ref, wb_ref, wc_ref, wh_ref, cw_ref, wo_ref, n1_ref,
                        n2_ref, o_ref, st_ref, n1_bf, n2_bf, z_ref, xbuf_ref, sem, *,
                        n_i, tm, rb):
    i = pl.program_id(0)
    c = pl.program_id(1)
    _fetch_row_tile(x_hbm, xbuf_ref, sem, tm, n_i)
    n1_bf[...] = n1_ref[...].astype(BF16)
    n2_bf[...] = n2_ref[...].astype(BF16)

    @pl.when(c == 0)
    def _():
        for k in range(tm // rb):
            rows = slice(k * rb, (k + 1) * rb)
            z_ref[rows] = _rmsnorm(xbuf_ref[rows], g_ref[...]).astype(BF16)

    @pl.when(i == 0)
    def _():
        st_ref[c] = hist_ref[0]

    def step(base_ref):
        z = z_ref[...]
        u = _dot(z, wc_ref[...]) * _dot(z, wh_ref[...])
        b = _dot(z, wb_ref[...])
        v = _conv_gate(b, u, st_ref[c], cw_ref[...])
        st_ref[c] = u[u.shape[0] - CONV_PAD:]
        o_ref[...] = base_ref[...] + _dot(v, wo_ref[...])

    pl.when(c == 0)(lambda: step(xbuf_ref))
    pl.when(c > 0)(lambda: step(o_ref))


def _conv_prompt(x, hist, g, wb, wc, wh, cw, w_out, nxt, *, tm, tn, rb):
    m, d = x.shape
    n_i, n_c = m // tm, d // tn
    assert n_i * tm == m and n_c * tn == d and n_c >= 2 and tm % rb == 0 and rb % SUBLANES == 0
    w_spec = pl.BlockSpec((d, tn), lambda i, c: (0, c))
    n1_in, n1_out, n1_shape = _slab_specs(nxt[0], nxt[2], n_i * n_c, n_c)
    n2_in, n2_out, n2_shape = _slab_specs(nxt[1], nxt[2], n_i * n_c, n_c)
    y, st, n1_bf, n2_bf = pl.pallas_call(
        functools.partial(_conv_prompt_kernel, n_i=n_i, tm=tm, rb=rb),
        grid=(n_i, n_c),
        in_specs=[
            pl.BlockSpec(memory_space=pl.ANY),
            pl.BlockSpec((1, CONV_PAD, tn), lambda i, c: (0, 0, c)),
            pl.BlockSpec((1, d), lambda i, c: (0, 0)),
            w_spec, w_spec, w_spec,
            pl.BlockSpec((CONV_WIDTH, tn), lambda i, c: (0, c)),
            pl.BlockSpec((tn, d), lambda i, c: (c, 0)),
            n1_in, n2_in,
        ],
        out_specs=[
            pl.BlockSpec((tm, d), lambda i, c: (i, 0)),
            pl.BlockSpec((n_c, CONV_PAD, tn), lambda i, c: (0, 0, 0)),
            n1_out, n2_out,
        ],
        out_shape=[
            jax.ShapeDtypeStruct((m, d), F32),
            jax.ShapeDtypeStruct((n_c, CONV_PAD, tn), F32),
            n1_shape, n2_shape,
        ],
        scratch_shapes=[pltpu.VMEM((tm, d), BF16), pltpu.VMEM((tm, d), F32),
                        pltpu.SemaphoreType.DMA(())],
        compiler_params=_PARAMS,
        name="conv_prompt",
    )(x, hist, g, wb, wc, wh, cw, w_out, nxt[0], nxt[1])
    return y, st.transpose(1, 0, 2).reshape(1, CONV_PAD, d), n1_bf, n2_bf


def _ffn_prompt_kernel(*refs, pool, final, n_i, tm, rb):
    x_hbm, g_ref, w1_ref, w2_ref, fin_ref = refs[:5]
    if pool:
        mg_ref, hist_ref, pw_ref, ps_ref, o_ref, st_ref = refs[5:11]
    else:
        n1_ref, n2_ref, o_ref, n1_bf, n2_bf = refs[5:10]
    z_ref, xbuf_ref, sem = refs[-3:]
    i = pl.program_id(0)
    f = pl.program_id(1)
    n_f = pl.num_programs(1)
    _fetch_row_tile(x_hbm, xbuf_ref, sem, tm, n_i)
    if not pool:
        n1_bf[...] = n1_ref[...].astype(BF16)
        n2_bf[...] = n2_ref[...].astype(BF16)

    if pool:
        @pl.when((i == 0) & (f == 0))
        def _():
            st_ref[...] = hist_ref[...]

    @pl.when(f == 0)
    def _():
        t = lax.broadcasted_iota(jnp.int32, (rb, 1), 0)
        for k in range(tm // rb):
            rows = slice(k * rb, (k + 1) * rb)
            x = xbuf_ref[rows]
            if pool:
                zm = _rmsnorm(x, mg_ref[...])
                frames = (t + (i * tm + (k * rb + 1))).astype(F32)
                x = _pool_mix(x, zm, st_ref[0], frames, pw_ref, ps_ref)
                st_ref[0] = zm[rb - POOL_PAD:]
                o_ref[rows] = x
            z_ref[rows] = _rmsnorm(x, g_ref[...]).astype(BF16)

    def step(base_ref):
        h = jnp.maximum(_dot(z_ref[...], w1_ref[...]), 0.0)
        o_ref[...] = base_ref[...] + _dot((h * h).astype(BF16), w2_ref[...])

    if pool:
        step(o_ref)
    else:
        pl.when(f == 0)(lambda: step(xbuf_ref))
        pl.when(f > 0)(lambda: step(o_ref))

    if final:
        @pl.when(f == n_f - 1)
        def _():
            o_ref[...] = _rmsnorm(o_ref[...], fin_ref[...])


def _ffn_prompt(x, g, w1, w2, fin, *, tm, tf, rb, final, pool_args=None, nxt=None):
    m, d = x.shape
    d_ff = w1.shape[2]
    n_i, n_f = m // tm, d_ff // tf
    assert n_i * tm == m and n_f * tf == d_ff and n_f >= 2
    assert tm % rb == 0 and rb % SUBLANES == 0 and rb >= POOL_PAD
    pool = pool_args is not None
    assert pool != (nxt is not None)
    row = pl.BlockSpec((1, d), lambda i, f: (0, 0))
    in_specs = [
        pl.BlockSpec(memory_space=pl.ANY),
        row,
        pl.BlockSpec((None, d, tf), lambda i, f: (0, 0, f)),
        pl.BlockSpec((None, tf, d), lambda i, f: (0, f, 0)),
        row,
    ]
    out_specs = [pl.BlockSpec((tm, d), lambda i, f: (i, 0))]
    out_shape = [jax.ShapeDtypeStruct((m, d), F32)]
    args = [x, g, w1, w2, fin]
    if pool:
        st_spec = pl.BlockSpec((1, POOL_PAD, d), lambda i, f: (0, 0, 0))
        in_specs += [row, st_spec,
                     pl.BlockSpec(pool_args[2].shape, lambda i, f: (0, 0, 0), pipeline_mode=_ONCE),
                     row]
        out_specs.append(st_spec)
        out_shape.append(jax.ShapeDtypeStruct((1, POOL_PAD, d), F32))
        args += list(pool_args)
    else:
        n1_in, n1_out, n1_shape = _slab_specs(nxt[0], nxt[2], n_i * n_f, n_f)
        n2_in, n2_out, n2_shape = _slab_specs(nxt[1], nxt[2], n_i * n_f, n_f)
        in_specs += [n1_in, n2_in]
        out_specs += [n1_out, n2_out]
        out_shape += [n1_shape, n2_shape]
        args += [nxt[0], nxt[1]]
    return pl.pallas_call(
        functools.partial(_ffn_prompt_kernel, pool=pool, final=final, n_i=n_i, tm=tm, rb=rb),
        grid=(n_i, n_f),
        in_specs=in_specs,
        out_specs=out_specs,
        out_shape=out_shape,
        scratch_shapes=[pltpu.VMEM((tm, d), BF16), pltpu.VMEM((tm, d), F32),
                        pltpu.SemaphoreType.DMA(())],
        compiler_params=_PARAMS,
        name="pool_ffn_prompt" if pool else "ffn_prompt",
    )(*args)


def _pad_hist(hist, pad):
    b, h, d = hist.shape
    return jnp.concatenate([jnp.zeros((b, pad - h, d), hist.dtype), hist], axis=1)


def _states(conv_st, pool_st):
    return (conv_st[None, :, CONV_PAD - CONV_HIST:, :], pool_st[None, :, POOL_PAD - POOL_HIST:, :])


def kernel(x_prompt, x_sample, cache_conv, cache_pool, mix_norm, ffn_norm, conv_w_in, conv_w,
           conv_w_out, pool_w, pool_scale, ffn_w1, ffn_w2, final_norm):
    depth, d = mix_norm.shape
    assert depth == 2 and conv_w_in.shape[0] == 1 and pool_w.shape[0] == 1
    bp, lp, _ = x_prompt.shape
    bs, ls, _ = x_sample.shape
    assert bp == 1
    mix_g0, mix_g1 = mix_norm[0:1], mix_norm[1:2]
    ffn_g0, ffn_g1 = ffn_norm[0:1], ffn_norm[1:2]
    fin = final_norm.reshape(1, d)
    ps = pool_scale[0:1]

    xs = x_sample.reshape(bs * ls, d)
    xp = x_prompt.reshape(lp, d)
    ffn_tiles = dict(tm=1024, tf=1024, rb=256)

    xs, conv_s, wb, wc, wh, wo = _conv_sample(
        xs, _pad_hist(cache_conv[0], CONV_PAD), mix_g0, conv_w_in[0], conv_w[0], conv_w_out[0],
        tn=256, nseg=bs)
    xp, conv_p, w1_0, w2_0 = _conv_prompt(
        xp, jnp.zeros((1, CONV_PAD, d), F32), mix_g0, wb, wc, wh, conv_w[0], wo,
        (ffn_w1, ffn_w2, 0), tm=512, tn=512, rb=256)
    xs, = _ffn_sample(xs, ffn_g0, w1_0, w2_0, fin, tf=1024, final=False, nseg=bs)
    xp, w1_1, w2_1 = _ffn_prompt(xp, ffn_g0, w1_0, w2_0, fin, final=False, **ffn_tiles,
                                 nxt=(ffn_w1, ffn_w2, 1))
    y_s, pool_s, pw = _ffn_sample(
        xs, ffn_g1, w1_1, w2_1, fin, tf=1024, final=True, nseg=bs,
        pool_args=(mix_g1, _pad_hist(cache_pool[0], POOL_PAD), pool_w[0], ps), pos0=PAST_LEN)
    y_p, pool_p = _ffn_prompt(xp, ffn_g1, w1_1, w2_1, fin, final=True, **ffn_tiles,
                              pool_args=(mix_g1, jnp.zeros((1, POOL_PAD, d), F32), pw, ps))

    conv_state_p, pool_state_p = _states(conv_p, pool_p)
    conv_state_s, pool_state_s = _states(conv_s, pool_s)
    return (y_p.reshape(bp, lp, d), y_s.reshape(bs, ls, d), conv_state_p, pool_state_p,
            conv_state_s, pool_state_s)
```

```python
import functools

import jax
import jax.numpy as jnp
from jax import lax
from jax.experimental import pallas as pl
from jax.experimental.pallas import tpu as pltpu

EPS = 1e-6
CONV_WIDTH = 3
CONV_HIST = CONV_WIDTH - 1
POOL_WINDOWS = (2, 4, 8, 16)
POOL_HIST = max(POOL_WINDOWS) - 1
PAST_LEN = 1024
SUBLANES = 8
CONV_PAD = SUBLANES
POOL_PAD = 2 * SUBLANES
VMEM_LIMIT = 56 * 1024 * 1024

BF16 = jnp.bfloat16
F32 = jnp.float32
_PARAMS = pltpu.CompilerParams(dimension_semantics=("arbitrary", "arbitrary"),
                               vmem_limit_bytes=VMEM_LIMIT)
_ONCE = pl.Buffered(1)


def _rmsnorm(x, g):
    r = lax.rsqrt(jnp.mean(x * x, axis=-1, keepdims=True) + EPS)
    return (x * r) * g


def _dot(a, b):
    return jnp.dot(a, b, preferred_element_type=F32)


def _conv_gate(b, u, hist, cw):
    ext = jnp.concatenate([hist, u], axis=0)
    p1 = pltpu.roll(ext, 1, 0)[CONV_PAD:]
    p2 = pltpu.roll(ext, 2, 0)[CONV_PAD:]
    conv = p2 * cw[0:1] + p1 * cw[1:2] + u * cw[2:3]
    return (b * conv).astype(BF16)


def _pool_mix(x, z, hist, frames, pw_ref, ps_ref):
    r, d = x.shape
    gc = d // len(POOL_WINDOWS)
    outs = []
    for gi, w in enumerate(POOL_WINDOWS):
        cols = slice(gi * gc, (gi + 1) * gc)
        zs = z[:, cols]
        a = jnp.concatenate([hist[:, cols], zs], axis=0)
        shift = 1
        while shift < min(w, SUBLANES):
            a = a + pltpu.roll(a, shift, 0)
            shift *= 2
        a = a[SUBLANES:]
        a = a[SUBLANES:] + a[:r] if w > SUBLANES else a[SUBLANES:]
        pooled = a * (1.0 / jnp.minimum(frames, float(w))) - zs
        y = _dot(pooled.astype(BF16), pw_ref[gi]) * ps_ref[:, cols]
        outs.append(x[:, cols] + y)
    return jnp.concatenate(outs, axis=1)


def _cast(w_ref, w_bf_ref):
    w_bf_ref[...] = w_ref[...].astype(BF16)
    return w_bf_ref[...]


def _conv_sample_kernel(x_ref, hist_ref, g_ref, wb_ref, wc_ref, wh_ref, cw_ref, wo_ref,
                        o_ref, st_ref, wb_bf, wc_bf, wh_bf, wo_bf, z_ref, *, nseg, seg):
    c = pl.program_id(1)

    @pl.when(c == 0)
    def _():
        x = x_ref[...]
        z_ref[...] = _rmsnorm(x, g_ref[...]).astype(BF16)
        o_ref[...] = x

    z = z_ref[...]
    b = _dot(z, _cast(wb_ref, wb_bf))
    u = _dot(z, _cast(wc_ref, wc_bf)) * _dot(z, _cast(wh_ref, wh_bf))
    cw = cw_ref[...]
    vs = []
    for s in range(nseg):
        rows = slice(s * seg, (s + 1) * seg)
        vs.append(_conv_gate(b[rows], u[rows], hist_ref[s], cw))
        st_ref[s] = u[(s + 1) * seg - CONV_PAD:(s + 1) * seg]
    o_ref[...] += _dot(jnp.concatenate(vs, axis=0), _cast(wo_ref, wo_bf))


def _conv_sample(x, hist, g, w_in, cw, w_out, *, tn, nseg):
    m, d = x.shape
    n_c, seg = d // tn, m // nseg
    assert n_c * tn == d and seg * nseg == m and seg >= CONV_PAD
    st_spec = pl.BlockSpec((nseg, CONV_PAD, tn), lambda i, c: (0, 0, c))
    return pl.pallas_call(
        functools.partial(_conv_sample_kernel, nseg=nseg, seg=seg),
        grid=(1, n_c),
        in_specs=[
            pl.BlockSpec((m, d), lambda i, c: (0, 0)),
            st_spec,
            pl.BlockSpec((1, d), lambda i, c: (0, 0)),
            pl.BlockSpec((d, tn), lambda i, c: (0, c)),
            pl.BlockSpec((d, tn), lambda i, c: (0, n_c + c)),
            pl.BlockSpec((d, tn), lambda i, c: (0, 2 * n_c + c)),
            pl.BlockSpec((CONV_WIDTH, tn), lambda i, c: (0, c)),
            pl.BlockSpec((tn, d), lambda i, c: (c, 0)),
        ],
        out_specs=[
            pl.BlockSpec((m, d), lambda i, c: (0, 0)),
            st_spec,
            *[pl.BlockSpec((d, tn), lambda i, c: (0, c))] * 3,
            pl.BlockSpec((tn, d), lambda i, c: (c, 0)),
        ],
        out_shape=[
            jax.ShapeDtypeStruct((m, d), F32),
            jax.ShapeDtypeStruct((nseg, CONV_PAD, d), F32),
            *[jax.ShapeDtypeStruct((d, d), BF16)] * 4,
        ],
        scratch_shapes=[pltpu.VMEM((m, d), BF16)],
        compiler_params=_PARAMS,
        name="conv_sample",
    )(x, hist, g, w_in, w_in, w_in, cw, w_out)


def _ffn_sample_kernel(*refs, pool, final, nseg, seg, pos0):
    x_ref, g_ref, w1_ref, w2_ref, fin_ref = refs[:5]
    if pool:
        mg_ref, hist_ref, pw_ref, ps_ref, o_ref, st_ref, pw_bf, z_ref = refs[5:]
    else:
        o_ref, z_ref = refs[5:]
    f = pl.program_id(1)

    @pl.when(f == 0)
    def _():
        x = x_ref[...]
        if pool:
            pw_bf[...] = pw_ref[...].astype(BF16)
            zm = _rmsnorm(x, mg_ref[...])
            frames = (lax.broadcasted_iota(jnp.int32, (seg, 1), 0) + (pos0 + 1)).astype(F32)
            xs = []
            for s in range(nseg):
                rows = slice(s * seg, (s + 1) * seg)
                xs.append(_pool_mix(x[rows], zm[rows], hist_ref[s], frames, pw_bf, ps_ref))
                st_ref[s] = zm[(s + 1) * seg - POOL_PAD:(s + 1) * seg]
            x = jnp.concatenate(xs, axis=0)
        o_ref[...] = x
        z_ref[...] = _rmsnorm(x, g_ref[...]).astype(BF16)

    h = jnp.maximum(_dot(z_ref[...], w1_ref[...]), 0.0)
    o_ref[...] += _dot((h * h).astype(BF16), w2_ref[...])

    if final:
        @pl.when(f == pl.num_programs(1) - 1)
        def _():
            o_ref[...] = _rmsnorm(o_ref[...], fin_ref[...])


def _ffn_sample(x, g, w1, w2, fin, *, tf, final, nseg, pool_args=None, pos0=0):
    m, d = x.shape
    d_ff = w1.shape[2]
    n_f, seg = d_ff // tf, m // nseg
    assert n_f * tf == d_ff and seg * nseg == m and seg >= POOL_PAD
    pool = pool_args is not None
    row = pl.BlockSpec((1, d), lambda i, f: (0, 0))
    in_specs = [
        pl.BlockSpec((m, d), lambda i, f: (0, 0)),
        row,
        pl.BlockSpec((None, d, tf), lambda i, f: (0, 0, f)),
        pl.BlockSpec((None, tf, d), lambda i, f: (0, f, 0)),
        row,
    ]
    out_specs = [pl.BlockSpec((m, d), lambda i, f: (0, 0))]
    out_shape = [jax.ShapeDtypeStruct((m, d), F32)]
    args = [x, g, w1, w2, fin]
    if pool:
        pw_shape = pool_args[2].shape
        st_spec = pl.BlockSpec((nseg, POOL_PAD, d), lambda i, f: (0, 0, 0))
        in_specs += [row, st_spec, pl.BlockSpec(pw_shape, lambda i, f: (0, 0, 0)), row]
        out_specs.append(st_spec)
        out_shape.append(jax.ShapeDtypeStruct((nseg, POOL_PAD, d), F32))
        args += list(pool_args)
        out_specs.append(pl.BlockSpec(pw_shape, lambda i, f: (0, 0, 0)))
        out_shape.append(jax.ShapeDtypeStruct(pw_shape, BF16))
    return pl.pallas_call(
        functools.partial(_ffn_sample_kernel, pool=pool, final=final, nseg=nseg, seg=seg,
                          pos0=pos0),
        grid=(1, n_f),
        in_specs=in_specs,
        out_specs=out_specs,
        out_shape=out_shape,
        scratch_shapes=[pltpu.VMEM((m, d), BF16)],
        compiler_params=_PARAMS,
        name="pool_ffn_sample" if pool else "ffn_sample",
    )(*args)


def _fetch_row_tile(x_hbm, xbuf_ref, sem, tm, n_i):
    i = pl.program_id(0)
    k = pl.program_id(1)

    def copy(tile):
        return pltpu.make_async_copy(x_hbm.at[pl.ds(tile * tm, tm)], xbuf_ref, sem)

    @pl.when((i == 0) & (k == 0))
    def _():
        copy(0).start()

    @pl.when(k == 0)
    def _():
        copy(i).wait()

    @pl.when((k == 1) & (i + 1 < n_i))
    def _():
        copy(i + 1).start()


def _slab_specs(w, layer, n_steps, n_k):
    _, r, c = w.shape
    rs = r // n_steps
    assert rs * n_steps == r and rs % (2 * SUBLANES) == 0
    return (pl.BlockSpec((None, rs, c), lambda i, k: (layer, i * n_k + k, 0)),
            pl.BlockSpec((None, rs, c), lambda i, k: (0, i * n_k + k, 0)),
            jax.ShapeDtypeStruct((1, r, c), BF16))


def _conv_prompt_kernel(x_hbm, hist_ref, g_ref, wb_ref, wc_ref, wh_ref, cw_ref, wo_ref, n1_ref,
                        n2_ref, o_ref, st_ref, n1_bf, n2_bf, z_ref, xbuf_ref, sem, *,
                        n_i, tm, rb):
    i = pl.program_id(0)
    c = pl.program_id(1)
    _fetch_row_tile(x_hbm, xbuf_ref, sem, tm, n_i)
    n1_bf[...] = n1_ref[...].astype(BF16)
    n2_bf[...] = n2_ref[...].astype(BF16)

    @pl.when(i == 0)
    def _():
        st_ref[c] = hist_ref[0]

    def conv_chunk(z, hist, base):
        u = _dot(z, wc_ref[...]) * _dot(z, wh_ref[...])
        b = _dot(z, wb_ref[...])
        v = _conv_gate(b, u, hist, cw_ref[...])
        return base + _dot(v, wo_ref[...]), u[u.shape[0] - CONV_PAD:]

    @pl.when(c == 0)
    def _():
        hist = st_ref[c]
        for k in range(tm // rb):
            rows = slice(k * rb, (k + 1) * rb)
            x = xbuf_ref[rows]
            z = _rmsnorm(x, g_ref[...]).astype(BF16)
            z_ref[rows] = z
            o_ref[rows], hist = conv_chunk(z, hist, x)
        st_ref[c] = hist

    @pl.when(c > 0)
    def _():
        o_ref[...], st_ref[c] = conv_chunk(z_ref[...], st_ref[c], o_ref[...])


def _conv_prompt(x, hist, g, wb, wc, wh, cw, w_out, nxt, *, tm, tn, rb):
    m, d = x.shape
    n_i, n_c = m // tm, d // tn
    assert n_i * tm == m and n_c * tn == d and n_c >= 2 and tm % rb == 0 and rb % SUBLANES == 0
    w_spec = pl.BlockSpec((d, tn), lambda i, c: (0, c))
    n1_in, n1_out, n1_shape = _slab_specs(nxt[0], nxt[2], n_i * n_c, n_c)
    n2_in, n2_out, n2_shape = _slab_specs(nxt[1], nxt[2], n_i * n_c, n_c)
    y, st, n1_bf, n2_bf = pl.pallas_call(
        functools.partial(_conv_prompt_kernel, n_i=n_i, tm=tm, rb=rb),
        grid=(n_i, n_c),
        in_specs=[
            pl.BlockSpec(memory_space=pl.ANY),
            pl.BlockSpec((1, CONV_PAD, tn), lambda i, c: (0, 0, c)),
            pl.BlockSpec((1, d), lambda i, c: (0, 0)),
            w_spec, w_spec, w_spec,
            pl.BlockSpec((CONV_WIDTH, tn), lambda i, c: (0, c)),
            pl.BlockSpec((tn, d), lambda i, c: (c, 0)),
            n1_in, n2_in,
        ],
        out_specs=[
            pl.BlockSpec((tm, d), lambda i, c: (i, 0)),
            pl.BlockSpec((n_c, CONV_PAD, tn), lambda i, c: (0, 0, 0)),
            n1_out, n2_out,
        ],
        out_shape=[
            jax.ShapeDtypeStruct((m, d), F32),
            jax.ShapeDtypeStruct((n_c, CONV_PAD, tn), F32),
            n1_shape, n2_shape,
        ],
        scratch_shapes=[pltpu.VMEM((tm, d), BF16), pltpu.VMEM((tm, d), F32),
                        pltpu.SemaphoreType.DMA(())],
        compiler_params=_PARAMS,
        name="conv_prompt",
    )(x, hist, g, wb, wc, wh, cw, w_out, nxt[0], nxt[1])
    return y, st.transpose(1, 0, 2).reshape(1, CONV_PAD, d), n1_bf, n2_bf


def _ffn_prompt_kernel(*refs, pool, final, n_i, tm, rb):
    x_hbm, g_ref, w1_ref, w2_ref, fin_ref = refs[:5]
    if pool:
        mg_ref, hist_ref, pw_ref, ps_ref, o_ref, st_ref = refs[5:11]
    else:
        n1_ref, n2_ref, o_ref, n1_bf, n2_bf = refs[5:10]
    z_ref, xbuf_ref, sem = refs[-3:]
    i = pl.program_id(0)
    f = pl.program_id(1)
    n_f = pl.num_programs(1)
    _fetch_row_tile(x_hbm, xbuf_ref, sem, tm, n_i)
    if not pool:
        n1_bf[...] = n1_ref[...].astype(BF16)
        n2_bf[...] = n2_ref[...].astype(BF16)

    if pool:
        @pl.when((i == 0) & (f == 0))
        def _():
            st_ref[...] = hist_ref[...]

    def ffn_chunk(z, base):
        h = jnp.maximum(_dot(z, w1_ref[...]), 0.0)
        return base + _dot((h * h).astype(BF16), w2_ref[...])

    @pl.when(f == 0)
    def _():
        t = lax.broadcasted_iota(jnp.int32, (rb, 1), 0)
        for k in range(tm // rb):
            rows = slice(k * rb, (k + 1) * rb)
            x = xbuf_ref[rows]
            if pool:
                zm = _rmsnorm(x, mg_ref[...])
                frames = (t + (i * tm + (k * rb + 1))).astype(F32)
                x = _pool_mix(x, zm, st_ref[0], frames, pw_ref, ps_ref)
                st_ref[0] = zm[rb - POOL_PAD:]
            z = _rmsnorm(x, g_ref[...]).astype(BF16)
            z_ref[rows] = z
            o_ref[rows] = ffn_chunk(z, x)

    @pl.when(f > 0)
    def _():
        o_ref[...] = ffn_chunk(z_ref[...], o_ref[...])

    if final:
        @pl.when(f == n_f - 1)
        def _():
            o_ref[...] = _rmsnorm(o_ref[...], fin_ref[...])


def _ffn_prompt(x, g, w1, w2, fin, *, tm, tf, rb, final, pool_args=None, nxt=None):
    m, d = x.shape
    d_ff = w1.shape[2]
    n_i, n_f = m // tm, d_ff // tf
    assert n_i * tm == m and n_f * tf == d_ff and n_f >= 2
    assert tm % rb == 0 and rb % SUBLANES == 0 and rb >= POOL_PAD
    pool = pool_args is not None
    assert pool != (nxt is not None)
    row = pl.BlockSpec((1, d), lambda i, f: (0, 0))
    in_specs = [
        pl.BlockSpec(memory_space=pl.ANY),
        row,
        pl.BlockSpec((None, d, tf), lambda i, f: (0, 0, f)),
        pl.BlockSpec((None, tf, d), lambda i, f: (0, f, 0)),
        row,
    ]
    out_specs = [pl.BlockSpec((tm, d), lambda i, f: (i, 0))]
    out_shape = [jax.ShapeDtypeStruct((m, d), F32)]
    args = [x, g, w1, w2, fin]
    if pool:
        st_spec = pl.BlockSpec((1, POOL_PAD, d), lambda i, f: (0, 0, 0))
        in_specs += [row, st_spec,
                     pl.BlockSpec(pool_args[2].shape, lambda i, f: (0, 0, 0), pipeline_mode=_ONCE),
                     row]
        out_specs.append(st_spec)
        out_shape.append(jax.ShapeDtypeStruct((1, POOL_PAD, d), F32))
        args += list(pool_args)
    else:
        n1_in, n1_out, n1_shape = _slab_specs(nxt[0], nxt[2], n_i * n_f, n_f)
        n2_in, n2_out, n2_shape = _slab_specs(nxt[1], nxt[2], n_i * n_f, n_f)
        in_specs += [n1_in, n2_in]
        out_specs += [n1_out, n2_out]
        out_shape += [n1_shape, n2_shape]
        args += [nxt[0], nxt[1]]
    return pl.pallas_call(
        functools.partial(_ffn_prompt_kernel, pool=pool, final=final, n_i=n_i, tm=tm, rb=rb),
        grid=(n_i, n_f),
        in_specs=in_specs,
        out_specs=out_specs,
        out_shape=out_shape,
        scratch_shapes=[pltpu.VMEM((tm, d), BF16), pltpu.VMEM((tm, d), F32),
                        pltpu.SemaphoreType.DMA(())],
        compiler_params=_PARAMS,
        name="pool_ffn_prompt" if pool else "ffn_prompt",
    )(*args)


def _pad_hist(hist, pad):
    b, h, d = hist.shape
    return jnp.concatenate([jnp.zeros((b, pad - h, d), hist.dtype), hist], axis=1)


def _states(conv_st, pool_st):
    return (conv_st[None, :, CONV_PAD - CONV_HIST:, :], pool_st[None, :, POOL_PAD - POOL_HIST:, :])


def kernel(x_prompt, x_sample, cache_conv, cache_pool, mix_norm, ffn_norm, conv_w_in, conv_w,
           conv_w_out, pool_w, pool_scale, ffn_w1, ffn_w2, final_norm):
    depth, d = mix_norm.shape
    assert depth == 2 and conv_w_in.shape[0] == 1 and pool_w.shape[0] == 1
    bp, lp, _ = x_prompt.shape
    bs, ls, _ = x_sample.shape
    assert bp == 1
    mix_g0, mix_g1 = mix_norm[0:1], mix_norm[1:2]
    ffn_g0, ffn_g1 = ffn_norm[0:1], ffn_norm[1:2]
    fin = final_norm.reshape(1, d)
    ps = pool_scale[0:1]

    xs = x_sample.reshape(bs * ls, d)
    xp = x_prompt.reshape(lp, d)
    ffn_tiles = dict(tm=1024, tf=1024, rb=256)

    xs, conv_s, wb, wc, wh, wo = _conv_sample(
        xs, _pad_hist(cache_conv[0], CONV_PAD), mix_g0, conv_w_in[0], conv_w[0], conv_w_out[0],
        tn=256, nseg=bs)
    xp, conv_p, w1_0, w2_0 = _conv_prompt(
        xp, jnp.zeros((1, CONV_PAD, d), F32), mix_g0, wb, wc, wh, conv_w[0], wo,
        (ffn_w1, ffn_w2, 0), tm=512, tn=512, rb=256)
    xs, = _ffn_sample(xs, ffn_g0, w1_0, w2_0, fin, tf=1024, final=False, nseg=bs)
    xp, w1_1, w2_1 = _ffn_prompt(xp, ffn_g0, w1_0, w2_0, fin, final=False, **ffn_tiles,
                                 nxt=(ffn_w1, ffn_w2, 1))
    y_s, pool_s, pw = _ffn_sample(
        xs, ffn_g1, w1_1, w2_1, fin, tf=1024, final=True, nseg=bs,
        pool_args=(mix_g1, _pad_hist(cache_pool[0], POOL_PAD), pool_w[0], ps), pos0=PAST_LEN)
    y_p, pool_p = _ffn_prompt(xp, ffn_g1, w1_1, w2_1, fin, final=True, **ffn_tiles,
                              pool_args=(mix_g1, jnp.zeros((1, POOL_PAD, d), F32), pw, ps))

    conv_state_p, pool_state_p = _states(conv_p, pool_p)
    conv_state_s, pool_state_s = _states(conv_s, pool_s)
    return (y_p.reshape(bp, lp, d), y_s.reshape(bs, ls, d), conv_state_p, pool_state_p,
            conv_state_s, pool_state_s)
```

```python
import functools

import jax
import jax.numpy as jnp
from jax import lax
from jax.experimental import pallas as pl
from jax.experimental.pallas import tpu as pltpu

EPS = 1e-6
CONV_WIDTH = 3
CONV_HIST = CONV_WIDTH - 1
POOL_WINDOWS = (2, 4, 8, 16)
POOL_HIST = max(POOL_WINDOWS) - 1
PAST_LEN = 1024
SUBLANES = 8
CONV_PAD = SUBLANES
POOL_PAD = 2 * SUBLANES
VMEM_LIMIT = 56 * 1024 * 1024

BF16 = jnp.bfloat16
F32 = jnp.float32
_PARAMS = pltpu.CompilerParams(dimension_semantics=("arbitrary", "arbitrary"),
                               vmem_limit_bytes=VMEM_LIMIT)
_ONCE = pl.Buffered(1)


def _rmsnorm(x, g):
    r = lax.rsqrt(jnp.mean(x * x, axis=-1, keepdims=True) + EPS)
    return (x * r) * g


def _dot(a, b):
    return jnp.dot(a, b, preferred_element_type=F32)


def _conv_gate(b, u, hist, cw):
    ext = jnp.concatenate([hist, u], axis=0)
    p1 = pltpu.roll(ext, 1, 0)[CONV_PAD:]
    p2 = pltpu.roll(ext, 2, 0)[CONV_PAD:]
    conv = p2 * cw[0:1] + p1 * cw[1:2] + u * cw[2:3]
    return (b * conv).astype(BF16)


def _pool_mix(x, z, hist, frames, pw_ref, ps_ref):
    r, d = x.shape
    gc = d // len(POOL_WINDOWS)
    outs = []
    for gi, w in enumerate(POOL_WINDOWS):
        cols = slice(gi * gc, (gi + 1) * gc)
        zs = z[:, cols]
        a = jnp.concatenate([hist[:, cols], zs], axis=0)
        shift = 1
        while shift < min(w, SUBLANES):
            a = a + pltpu.roll(a, shift, 0)
            shift *= 2
        a = a[SUBLANES:]
        a = a[SUBLANES:] + a[:r] if w > SUBLANES else a[SUBLANES:]
        pooled = a * (1.0 / jnp.minimum(frames, float(w))) - zs
        y = _dot(pooled.astype(BF16), pw_ref[gi]) * ps_ref[:, cols]
        outs.append(x[:, cols] + y)
    return jnp.concatenate(outs, axis=1)


def _cast(w_ref, w_bf_ref):
    w_bf_ref[...] = w_ref[...].astype(BF16)
    return w_bf_ref[...]


def _conv_sample_kernel(x_ref, hist_ref, g_ref, wb_ref, wc_ref, wh_ref, cw_ref, wo_ref,
                        o_ref, st_ref, wb_bf, wc_bf, wh_bf, wo_bf, z_ref, *, nseg, seg):
    c = pl.program_id(1)

    @pl.when(c == 0)
    def _():
        x = x_ref[...]
        z_ref[...] = _rmsnorm(x, g_ref[...]).astype(BF16)
        o_ref[...] = x

    z = z_ref[...]
    b = _dot(z, _cast(wb_ref, wb_bf))
    u = _dot(z, _cast(wc_ref, wc_bf)) * _dot(z, _cast(wh_ref, wh_bf))
    cw = cw_ref[...]
    vs = []
    for s in range(nseg):
        rows = slice(s * seg, (s + 1) * seg)
        vs.append(_conv_gate(b[rows], u[rows], hist_ref[s], cw))
        st_ref[s] = u[(s + 1) * seg - CONV_PAD:(s + 1) * seg]
    o_ref[...] += _dot(jnp.concatenate(vs, axis=0), _cast(wo_ref, wo_bf))


def _conv_sample(x, hist, g, w_in, cw, w_out, *, tn, nseg):
    m, d = x.shape
    n_c, seg = d // tn, m // nseg
    assert n_c * tn == d and seg * nseg == m and seg >= CONV_PAD
    st_spec = pl.BlockSpec((nseg, CONV_PAD, tn), lambda i, c: (0, 0, c))
    return pl.pallas_call(
        functools.partial(_conv_sample_kernel, nseg=nseg, seg=seg),
        grid=(1, n_c),
        in_specs=[
            pl.BlockSpec((m, d), lambda i, c: (0, 0)),
            st_spec,
            pl.BlockSpec((1, d), lambda i, c: (0, 0)),
            pl.BlockSpec((d, tn), lambda i, c: (0, c)),
            pl.BlockSpec((d, tn), lambda i, c: (0, n_c + c)),
            pl.BlockSpec((d, tn), lambda i, c: (0, 2 * n_c + c)),
            pl.BlockSpec((CONV_WIDTH, tn), lambda i, c: (0, c)),
            pl.BlockSpec((tn, d), lambda i, c: (c, 0)),
        ],
        out_specs=[
            pl.BlockSpec((m, d), lambda i, c: (0, 0)),
            st_spec,
            *[pl.BlockSpec((d, tn), lambda i, c: (0, c))] * 3,
            pl.BlockSpec((tn, d), lambda i, c: (c, 0)),
        ],
        out_shape=[
            jax.ShapeDtypeStruct((m, d), F32),
            jax.ShapeDtypeStruct((nseg, CONV_PAD, d), F32),
            *[jax.ShapeDtypeStruct((d, d), BF16)] * 4,
        ],
        scratch_shapes=[pltpu.VMEM((m, d), BF16)],
        compiler_params=_PARAMS,
        name="conv_sample",
    )(x, hist, g, w_in, w_in, w_in, cw, w_out)


def _ffn_sample_kernel(*refs, pool, final, nseg, seg, pos0):
    x_ref, g_ref, w1_ref, w2_ref, fin_ref = refs[:5]
    if pool:
        mg_ref, hist_ref, pw_ref, ps_ref, o_ref, st_ref, pw_bf, z_ref = refs[5:]
    else:
        o_ref, z_ref = refs[5:]
    f = pl.program_id(1)

    @pl.when(f == 0)
    def _():
        x = x_ref[...]
        if pool:
            pw_bf[...] = pw_ref[...].astype(BF16)
            zm = _rmsnorm(x, mg_ref[...])
            frames = (lax.broadcasted_iota(jnp.int32, (seg, 1), 0) + (pos0 + 1)).astype(F32)
            xs = []
            for s in range(nseg):
                rows = slice(s * seg, (s + 1) * seg)
                xs.append(_pool_mix(x[rows], zm[rows], hist_ref[s], frames, pw_bf, ps_ref))
                st_ref[s] = zm[(s + 1) * seg - POOL_PAD:(s + 1) * seg]
            x = jnp.concatenate(xs, axis=0)
        o_ref[...] = x
        z_ref[...] = _rmsnorm(x, g_ref[...]).astype(BF16)

    h = jnp.maximum(_dot(z_ref[...], w1_ref[...]), 0.0)
    o_ref[...] += _dot((h * h).astype(BF16), w2_ref[...])

    if final:
        @pl.when(f == pl.num_programs(1) - 1)
        def _():
            o_ref[...] = _rmsnorm(o_ref[...], fin_ref[...])


def _ffn_sample(x, g, w1, w2, fin, *, tf, final, nseg, pool_args=None, pos0=0):
    m, d = x.shape
    d_ff = w1.shape[2]
    n_f, seg = d_ff // tf, m // nseg
    assert n_f * tf == d_ff and seg * nseg == m and seg >= POOL_PAD
    pool = pool_args is not None
    row = pl.BlockSpec((1, d), lambda i, f: (0, 0))
    in_specs = [
        pl.BlockSpec((m, d), lambda i, f: (0, 0)),
        row,
        pl.BlockSpec((None, d, tf), lambda i, f: (0, 0, f)),
        pl.BlockSpec((None, tf, d), lambda i, f: (0, f, 0)),
        row,
    ]
    out_specs = [pl.BlockSpec((m, d), lambda i, f: (0, 0))]
    out_shape = [jax.ShapeDtypeStruct((m, d), F32)]
    args = [x, g, w1, w2, fin]
    if pool:
        pw_shape = pool_args[2].shape
        st_spec = pl.BlockSpec((nseg, POOL_PAD, d), lambda i, f: (0, 0, 0))
        in_specs += [row, st_spec, pl.BlockSpec(pw_shape, lambda i, f: (0, 0, 0)), row]
        out_specs.append(st_spec)
        out_shape.append(jax.ShapeDtypeStruct((nseg, POOL_PAD, d), F32))
        args += list(pool_args)
        out_specs.append(pl.BlockSpec(pw_shape, lambda i, f: (0, 0, 0)))
        out_shape.append(jax.ShapeDtypeStruct(pw_shape, BF16))
    return pl.pallas_call(
        functools.partial(_ffn_sample_kernel, pool=pool, final=final, nseg=nseg, seg=seg,
                          pos0=pos0),
        grid=(1, n_f),
        in_specs=in_specs,
        out_specs=out_specs,
        out_shape=out_shape,
        scratch_shapes=[pltpu.VMEM((m, d), BF16)],
        compiler_params=_PARAMS,
        name="pool_ffn_sample" if pool else "ffn_sample",
    )(*args)


def _fetch_row_tile(x_hbm, xbuf_ref, sem, tm, n_i):
    i = pl.program_id(0)
    k = pl.program_id(1)

    def copy(tile):
        return pltpu.make_async_copy(x_hbm.at[pl.ds(tile * tm, tm)], xbuf_ref, sem)

    @pl.when((i == 0) & (k == 0))
    def _():
        copy(0).start()

    @pl.when(k == 0)
    def _():
        copy(i).wait()

    @pl.when((k == 1) & (i + 1 < n_i))
    def _():
        copy(i + 1).start()


def _slab_specs(w, layer, n_steps, n_k):
    _, r, c = w.shape
    rs = r // n_steps
    assert rs * n_steps == r and rs % (2 * SUBLANES) == 0
    return (pl.BlockSpec((None, rs, c), lambda i, k: (layer, i * n_k + k, 0)),
            pl.BlockSpec((None, rs, c), lambda i, k: (0, i * n_k + k, 0)),
            jax.ShapeDtypeStruct((1, r, c), BF16))


def _conv_prompt_kernel(x_hbm, hist_ref, g_ref, wb_ref, wc_ref, wh_ref, cw_ref, wo_ref, n1_ref,
                        n2_ref, o_ref, st_ref, n1_bf, n2_bf, z_ref, xbuf_ref, sem, *,
                        n_i, tm, rb):
    i = pl.program_id(0)
    c = pl.program_id(1)
    _fetch_row_tile(x_hbm, xbuf_ref, sem, tm, n_i)

    @pl.when(i == 0)
    def _():
        st_ref[c] = hist_ref[0]

    def cast_slabs():
        n1_bf[...] = n1_ref[...].astype(BF16)
        n2_bf[...] = n2_ref[...].astype(BF16)

    def conv_chunk(z, hist, base):
        u = _dot(z, wc_ref[...]) * _dot(z, wh_ref[...])
        b = _dot(z, wb_ref[...])
        v = _conv_gate(b, u, hist, cw_ref[...])
        return base + _dot(v, wo_ref[...]), u[u.shape[0] - CONV_PAD:]

    @pl.when(c == 0)
    def _():
        cast_slabs()
        hist = st_ref[c]
        for k in range(tm // rb):
            rows = slice(k * rb, (k + 1) * rb)
            x = xbuf_ref[rows]
            z = _rmsnorm(x, g_ref[...]).astype(BF16)
            z_ref[rows] = z
            o_ref[rows], hist = conv_chunk(z, hist, x)
        st_ref[c] = hist

    @pl.when(c > 0)
    def _():
        cast_slabs()
        o_ref[...], st_ref[c] = conv_chunk(z_ref[...], st_ref[c], o_ref[...])


def _conv_prompt(x, hist, g, wb, wc, wh, cw, w_out, nxt, *, tm, tn, rb):
    m, d = x.shape
    n_i, n_c = m // tm, d // tn
    assert n_i * tm == m and n_c * tn == d and n_c >= 2 and tm % rb == 0 and rb % SUBLANES == 0
    w_spec = pl.BlockSpec((d, tn), lambda i, c: (0, c))
    n1_in, n1_out, n1_shape = _slab_specs(nxt[0], nxt[2], n_i * n_c, n_c)
    n2_in, n2_out, n2_shape = _slab_specs(nxt[1], nxt[2], n_i * n_c, n_c)
    y, st, n1_bf, n2_bf = pl.pallas_call(
        functools.partial(_conv_prompt_kernel, n_i=n_i, tm=tm, rb=rb),
        grid=(n_i, n_c),
        in_specs=[
            pl.BlockSpec(memory_space=pl.ANY),
            pl.BlockSpec((1, CONV_PAD, tn), lambda i, c: (0, 0, c)),
            pl.BlockSpec((1, d), lambda i, c: (0, 0)),
            w_spec, w_spec, w_spec,
            pl.BlockSpec((CONV_WIDTH, tn), lambda i, c: (0, c)),
            pl.BlockSpec((tn, d), lambda i, c: (c, 0)),
            n1_in, n2_in,
        ],
        out_specs=[
            pl.BlockSpec((tm, d), lambda i, c: (i, 0)),
            pl.BlockSpec((n_c, CONV_PAD, tn), lambda i, c: (0, 0, 0)),
            n1_out, n2_out,
        ],
        out_shape=[
            jax.ShapeDtypeStruct((m, d), F32),
            jax.ShapeDtypeStruct((n_c, CONV_PAD, tn), F32),
            n1_shape, n2_shape,
        ],
        scratch_shapes=[pltpu.VMEM((tm, d), BF16), pltpu.VMEM((tm, d), F32),
                        pltpu.SemaphoreType.DMA(())],
        compiler_params=_PARAMS,
        name="conv_prompt",
    )(x, hist, g, wb, wc, wh, cw, w_out, nxt[0], nxt[1])
    return y, st.transpose(1, 0, 2).reshape(1, CONV_PAD, d), n1_bf, n2_bf


def _ffn_prompt_kernel(*refs, pool, final, n_i, tm, rb):
    x_hbm, g_ref, w1_ref, w2_ref, fin_ref = refs[:5]
    if pool:
        mg_ref, hist_ref, pw_ref, ps_ref, o_ref, st_ref = refs[5:11]
    else:
        n1_ref, n2_ref, o_ref, n1_bf, n2_bf = refs[5:10]
    z_ref, xbuf_ref, sem = refs[-3:]
    i = pl.program_id(0)
    f = pl.program_id(1)
    n_f = pl.num_programs(1)
    _fetch_row_tile(x_hbm, xbuf_ref, sem, tm, n_i)

    if pool:
        @pl.when((i == 0) & (f == 0))
        def _():
            st_ref[...] = hist_ref[...]

    def cast_slabs():
        if not pool:
            n1_bf[...] = n1_ref[...].astype(BF16)
            n2_bf[...] = n2_ref[...].astype(BF16)

    def ffn_chunk(z, base):
        h = jnp.maximum(_dot(z, w1_ref[...]), 0.0)
        return base + _dot((h * h).astype(BF16), w2_ref[...])

    @pl.when(f == 0)
    def _():
        cast_slabs()
        t = lax.broadcasted_iota(jnp.int32, (rb, 1), 0)
        for k in range(tm // rb):
            rows = slice(k * rb, (k + 1) * rb)
            x = xbuf_ref[rows]
            if pool:
                zm = _rmsnorm(x, mg_ref[...])
                frames = (t + (i * tm + (k * rb + 1))).astype(F32)
                x = _pool_mix(x, zm, st_ref[0], frames, pw_ref, ps_ref)
                st_ref[0] = zm[rb - POOL_PAD:]
            z = _rmsnorm(x, g_ref[...]).astype(BF16)
            z_ref[rows] = z
            o_ref[rows] = ffn_chunk(z, x)

    @pl.when(f > 0)
    def _():
        cast_slabs()
        o_ref[...] = ffn_chunk(z_ref[...], o_ref[...])

    if final:
        @pl.when(f == n_f - 1)
        def _():
            o_ref[...] = _rmsnorm(o_ref[...], fin_ref[...])


def _ffn_prompt(x, g, w1, w2, fin, *, tm, tf, rb, final, pool_args=None, nxt=None):
    m, d = x.shape
    d_ff = w1.shape[2]
    n_i, n_f = m // tm, d_ff // tf
    assert n_i * tm == m and n_f * tf == d_ff and n_f >= 2
    assert tm % rb == 0 and rb % SUBLANES == 0 and rb >= POOL_PAD
    pool = pool_args is not None
    assert pool != (nxt is not None)
    row = pl.BlockSpec((1, d), lambda i, f: (0, 0))
    in_specs = [
        pl.BlockSpec(memory_space=pl.ANY),
        row,
        pl.BlockSpec((None, d, tf), lambda i, f: (0, 0, f)),
        pl.BlockSpec((None, tf, d), lambda i, f: (0, f, 0)),
        row,
    ]
    out_specs = [pl.BlockSpec((tm, d), lambda i, f: (i, 0))]
    out_shape = [jax.ShapeDtypeStruct((m, d), F32)]
    args = [x, g, w1, w2, fin]
    if pool:
        st_spec = pl.BlockSpec((1, POOL_PAD, d), lambda i, f: (0, 0, 0))
        in_specs += [row, st_spec,
                     pl.BlockSpec(pool_args[2].shape, lambda i, f: (0, 0, 0), pipeline_mode=_ONCE),
                     row]
        out_specs.append(st_spec)
        out_shape.append(jax.ShapeDtypeStruct((1, POOL_PAD, d), F32))
        args += list(pool_args)
    else:
        n1_in, n1_out, n1_shape = _slab_specs(nxt[0], nxt[2], n_i * n_f, n_f)
        n2_in, n2_out, n2_shape = _slab_specs(nxt[1], nxt[2], n_i * n_f, n_f)
        in_specs += [n1_in, n2_in]
        out_specs += [n1_out, n2_out]
        out_shape += [n1_shape, n2_shape]
        args += [nxt[0], nxt[1]]
    return pl.pallas_call(
        functools.partial(_ffn_prompt_kernel, pool=pool, final=final, n_i=n_i, tm=tm, rb=rb),
        grid=(n_i, n_f),
        in_specs=in_specs,
        out_specs=out_specs,
        out_shape=out_shape,
        scratch_shapes=[pltpu.VMEM((tm, d), BF16), pltpu.VMEM((tm, d), F32),
                        pltpu.SemaphoreType.DMA(())],
        compiler_params=_PARAMS,
        name="pool_ffn_prompt" if pool else "ffn_prompt",
    )(*args)


def _pad_hist(hist, pad):
    b, h, d = hist.shape
    return jnp.concatenate([jnp.zeros((b, pad - h, d), hist.dtype), hist], axis=1)


def _states(conv_st, pool_st):
    return (conv_st[None, :, CONV_PAD - CONV_HIST:, :], pool_st[None, :, POOL_PAD - POOL_HIST:, :])


def kernel(x_prompt, x_sample, cache_conv, cache_pool, mix_norm, ffn_norm, conv_w_in, conv_w,
           conv_w_out, pool_w, pool_scale, ffn_w1, ffn_w2, final_norm):
    depth, d = mix_norm.shape
    assert depth == 2 and conv_w_in.shape[0] == 1 and pool_w.shape[0] == 1
    bp, lp, _ = x_prompt.shape
    bs, ls, _ = x_sample.shape
    assert bp == 1
    mix_g0, mix_g1 = mix_norm[0:1], mix_norm[1:2]
    ffn_g0, ffn_g1 = ffn_norm[0:1], ffn_norm[1:2]
    fin = final_norm.reshape(1, d)
    ps = pool_scale[0:1]

    xs = x_sample.reshape(bs * ls, d)
    xp = x_prompt.reshape(lp, d)
    ffn_tiles = dict(tm=1024, tf=1024, rb=256)

    xs, conv_s, wb, wc, wh, wo = _conv_sample(
        xs, _pad_hist(cache_conv[0], CONV_PAD), mix_g0, conv_w_in[0], conv_w[0], conv_w_out[0],
        tn=256, nseg=bs)
    xp, conv_p, w1_0, w2_0 = _conv_prompt(
        xp, jnp.zeros((1, CONV_PAD, d), F32), mix_g0, wb, wc, wh, conv_w[0], wo,
        (ffn_w1, ffn_w2, 0), tm=512, tn=512, rb=256)
    xs, = _ffn_sample(xs, ffn_g0, w1_0, w2_0, fin, tf=1024, final=False, nseg=bs)
    xp, w1_1, w2_1 = _ffn_prompt(xp, ffn_g0, w1_0, w2_0, fin, final=False, **ffn_tiles,
                                 nxt=(ffn_w1, ffn_w2, 1))
    y_s, pool_s, pw = _ffn_sample(
        xs, ffn_g1, w1_1, w2_1, fin, tf=1024, final=True, nseg=bs,
        pool_args=(mix_g1, _pad_hist(cache_pool[0], POOL_PAD), pool_w[0], ps), pos0=PAST_LEN)
    y_p, pool_p = _ffn_prompt(xp, ffn_g1, w1_1, w2_1, fin, final=True, **ffn_tiles,
                              pool_args=(mix_g1, jnp.zeros((1, POOL_PAD, d), F32), pw, ps))

    conv_state_p, pool_state_p = _states(conv_p, pool_p)
    conv_state_s, pool_state_s = _states(conv_s, pool_s)
    return (y_p.reshape(bp, lp, d), y_s.reshape(bs, ls, d), conv_state_p, pool_state_p,
            conv_state_s, pool_state_s)
```
